```python
import math
import jax, jax.numpy as jnp
from jax import lax
import numpy as np

D_MODEL = 2048
BATCH = 4
SEQ = 8192
DEPTH = 2
DEC_BATCH = 1
DEC_SEQ = 8192
PAST_LEN = 128

RW_HEAD_SIZE = 64
RW_WIDTH = D_MODEL
RW_HEADS = RW_WIDTH // RW_HEAD_SIZE
R_DECAY = 96
R_AAA = 96
R_MV = 64
RW_GN_EPS = 64e-5
MB_WIDTH = 2 * D_MODEL
MB_HEAD_DIM = 64
MB_HEADS = MB_WIDTH // MB_HEAD_DIM
MB_GROUPS = 8
MB_STATE = 128
MB_CONV_CH = MB_WIDTH + 2 * MB_GROUPS * MB_STATE
CONV_W = 7
CHUNK = 128
N_DIR = 2
N_IN = 4 * RW_WIDTH + MB_WIDTH + MB_CONV_CH + N_DIR * MB_HEADS + 2 * D_MODEL
SPLITS = [3 * RW_WIDTH, 4 * RW_WIDTH, 4 * RW_WIDTH + MB_WIDTH, 4 * RW_WIDTH + MB_WIDTH + MB_CONV_CH, 4 * RW_WIDTH + MB_WIDTH + MB_CONV_CH + N_DIR * MB_HEADS]
ALPHA = (2 * DEPTH) ** 0.25
BETA = (8 * DEPTH) ** -0.25
LN_EPS = 1e-5
RMS_EPS = 1e-5
F32 = jnp.float32

kernel_name = 'hybrid_rwkv7_mamba2_bidir_encoder'


def layer_norm(x, g, b):
    xf = x.astype(F32)
    mu = jnp.mean(xf, -1, keepdims=True)
    xc = xf - mu
    var = jnp.mean(xc * xc, -1, keepdims=True)
    return xc * lax.rsqrt(var + LN_EPS) * g.astype(F32) + b.astype(F32)


def centred_delta(t):
    tp = jnp.pad(t, ((0, 0), (1, 1), (0, 0)))
    return 0.5 * (tp[:, :-2] + tp[:, 2:]) - t


def flip_t(t):
    return jnp.flip(t, axis=1)


def rwkv7_scan(r, w, k, v, a_vec, b_vec):
    bsz, _, nh, n = r.shape

    def step(S, inp):
        r_t, w_t, k_t, v_t, a_t, b_t = inp
        sa = jnp.einsum('bhij,bhj->bhi', S, a_t)
        S = S * w_t[:, :, None, :] + sa[..., None] * b_t[:, :, None, :] + v_t[..., None] * k_t[:, :, None, :]
        return S, jnp.einsum('bhij,bhj->bhi', S, r_t)

    xs = tuple(jnp.moveaxis(t.astype(F32), 1, 0) for t in (r, w, k, v, a_vec, b_vec))
    s0 = jnp.zeros((bsz, nh, n, n), F32)
    _, ys = lax.scan(step, s0, xs, unroll=4)
    return jnp.moveaxis(ys, 0, 1)


def segsum_exp(a_cum):
    l = a_cum.shape[-1]
    diff = a_cum[..., :, None] - a_cum[..., None, :]
    mask = jnp.tril(jnp.ones((l, l), dtype=bool))
    return jnp.exp(jnp.where(mask, diff, -jnp.inf))


def ssd_chunked(xh, dt, A, Bm, Cm):
    b, T, H, P = xh.shape
    G, N = Bm.shape[-2], Bm.shape[-1]
    E = H // G
    c = T // CHUNK
    xh = xh.astype(F32)
    dt = dt.astype(F32)
    X = (xh * dt[..., None]).reshape(b, c, CHUNK, G, E, P)
    Adt = (A.astype(F32) * dt).reshape(b, c, CHUNK, G, E).transpose(0, 3, 4, 1, 2)
    Bc = Bm.astype(F32).reshape(b, c, CHUNK, G, N)
    Cc = Cm.astype(F32).reshape(b, c, CHUNK, G, N)
    A_cum = jnp.cumsum(Adt, axis=-1)
    L = segsum_exp(A_cum)
    CB = jnp.einsum('bclgn,bcsgn->bgcls', Cc, Bc)
    Wd = L * CB[:, :, None]
    y_diag = jnp.einsum('bgecls,bcsgep->bclgep', Wd, X)
    decay_states = jnp.exp(A_cum[..., -1:] - A_cum)
    states = jnp.einsum('bclgn,bgecl,bclgep->bcgepn', Bc, decay_states, X)
    totals = jnp.pad(A_cum[..., -1], ((0, 0), (0, 0), (0, 0), (1, 0)))
    decay_chunk = segsum_exp(jnp.cumsum(totals, axis=-1))
    states_cat = jnp.concatenate([jnp.zeros_like(states[:, :1]), states], axis=1)
    new_states = jnp.einsum('bgezc,bcgepn->bzgepn', decay_chunk, states_cat)
    prev = new_states[:, :-1]
    y_off = jnp.einsum('bclgn,bcgepn,bgecl->bclgep', Cc, prev, jnp.exp(A_cum))
    return (y_diag + y_off).reshape(b, T, H, P)


def centred_dwconv(t, wk, bk):
    y = lax.conv_general_dilated(t.astype(F32), wk.astype(F32)[:, None, :], window_strides=(1,),
                                 padding=((CONV_W // 2, CONV_W // 2),),
                                 dimension_numbers=('NWC', 'WIO', 'NWC'),
                                 feature_group_count=t.shape[-1])
    return y + bk.astype(F32)


def rwkv7_branch(h, p_rkv, g_rw, v_first, vres, mu_rkv, mu_wa, w0, w1, w2, a0, a1, a2, k_k, k_a, r_k, lnx_g, lnx_b):
    bsz, T, _ = h.shape
    hs = (bsz, T, RW_HEADS, RW_HEAD_SIZE)
    p_rkv = p_rkv + mu_rkv.reshape(-1) * centred_delta(p_rkv)
    r, k, v = (t.reshape(hs) for t in jnp.split(p_rkv, 3, axis=-1))
    hd = centred_delta(h)
    xw = h + hd * mu_wa[0]
    xa = h + hd * mu_wa[1]
    w_lora = jnp.einsum('zbtr,zrc->zbtc', jnp.tanh(jnp.einsum('btd,zdr->zbtr', xw, w1)), w2)
    w_log = -jax.nn.softplus(-(w0[:, None, None, :] + w_lora).astype(F32)) - 0.5
    decay = jnp.exp(-jnp.exp(w_log)).reshape((N_DIR,) + hs)
    a = jax.nn.sigmoid((a0[:, None, None, :] + jnp.einsum('zbtr,zrc->zbtc', jnp.einsum('btd,zdr->zbtr', xa, a1), a2)).astype(F32)).reshape((N_DIR,) + hs)
    if vres is None:
        v_first = v
    else:
        mu_v, v0, v1, v2 = vres
        xv = h + hd * mu_v
        vg = jax.nn.sigmoid(v0 + jnp.einsum('btr,rc->btc', jnp.einsum('btd,dr->btr', xv, v1), v2)).reshape(hs)
        v = v + (v_first - v) * vg
    kk = k * k_k.reshape(RW_HEADS, RW_HEAD_SIZE)
    kk = kk / jnp.maximum(jnp.sqrt(jnp.sum(kk * kk, -1, keepdims=True)), 1e-12)
    k_dir = k[None] * (1.0 + (a - 1.0) * k_a.reshape(RW_HEADS, RW_HEAD_SIZE))
    y_f = rwkv7_scan(r, decay[0], k_dir[0], v, -kk, kk * a[0])
    y_b = flip_t(rwkv7_scan(flip_t(r), flip_t(decay[1]), flip_t(k_dir[1]), flip_t(v), flip_t(-kk), flip_t(kk * a[1])))
    y = y_f + y_b
    mu = jnp.mean(y, -1, keepdims=True)
    yc = y - mu
    var = jnp.mean(yc * yc, -1, keepdims=True)
    gn = (yc * lax.rsqrt(var + RW_GN_EPS)).reshape(bsz, T, RW_WIDTH) * lnx_g + lnx_b
    bonus = jnp.sum(r[None] * k_dir * r_k, axis=(0, -1))[..., None] * v
    o = (gn + bonus.reshape(bsz, T, RW_WIDTH)) * jax.nn.silu(g_rw)
    return o, v_first


def mamba2_branch(z, xbc, dt_raw, conv_w, conv_b, dt_bias, a_log, d_skip, norm_g):
    bsz, T, _ = z.shape
    xbc = jax.nn.silu(centred_dwconv(xbc, conv_w, conv_b))
    xm, Bm, Cm = jnp.split(xbc, [MB_WIDTH, MB_WIDTH + MB_GROUPS * MB_STATE], axis=-1)
    xh = xm.reshape(bsz, T, MB_HEADS, MB_HEAD_DIM)
    Bm = Bm.reshape(bsz, T, MB_GROUPS, MB_STATE)
    Cm = Cm.reshape(bsz, T, MB_GROUPS, MB_STATE)
    dt = jax.nn.softplus(dt_raw.reshape(bsz, T, N_DIR, MB_HEADS).astype(F32) + dt_bias.astype(F32))
    A = -jnp.exp(a_log.astype(F32))
    y_f = ssd_chunked(xh, dt[:, :, 0], A[0], Bm, Cm)
    y_b = flip_t(ssd_chunked(flip_t(xh), flip_t(dt[:, :, 1]), A[1], flip_t(Bm), flip_t(Cm)))
    y = (y_f + y_b + d_skip[:, None] * xh).reshape(bsz, T, MB_WIDTH) * jax.nn.silu(z)
    yg = y.reshape(bsz, T, MB_GROUPS, MB_WIDTH // MB_GROUPS)
    yg = yg * lax.rsqrt(jnp.mean(yg * yg, -1, keepdims=True) + RMS_EPS)
    return yg.reshape(bsz, T, MB_WIDTH) * norm_g


def trunk(x, ln_in_g, ln_in_b, w_in, mu_rkv, mu_wa, w0, w1, w2, a0, a1, a2, mu_v, v0, v1, v2,
          k_k, k_a, r_k, lnx_g, lnx_b, p_rw, conv_w, conv_b, dt_bias, a_log, d_skip, mb_norm_g,
          p_mb, w_out, ln_g, ln_b):
    out_dtype = x.dtype
    h = layer_norm(x, ln_in_g, ln_in_b)
    v_first = None
    for l in range(DEPTH):
        proj = jnp.einsum('btd,dn->btn', h, w_in[l])
        p_rkv, g_rw, z, xbc, dt_raw, gates = jnp.split(proj, SPLITS, axis=-1)
        vres = None if l == 0 else (mu_v[l - 1], v0[l - 1], v1[l - 1], v2[l - 1])
        o_rw, v_first = rwkv7_branch(h, p_rkv, g_rw, v_first, vres, mu_rkv[l], mu_wa[l], w0[l], w1[l], w2[l],
                                     a0[l], a1[l], a2[l], k_k[l], k_a[l], r_k[l], lnx_g[l], lnx_b[l])
        o_mb = mamba2_branch(z, xbc, dt_raw, conv_w[l], conv_b[l], dt_bias[l], a_log[l], d_skip[l], mb_norm_g[l])
        g_a, g_b = jnp.split(gates, 2, axis=-1)
        m = (jax.nn.sigmoid(g_a) * jnp.einsum('btc,cd->btd', o_rw, p_rw[l])
             + jax.nn.sigmoid(g_b) * jnp.einsum('btc,cd->btd', o_mb, p_mb[l]))
        out = jnp.einsum('btd,de->bte', m, w_out[l])
        h = layer_norm(ALPHA * h + out, ln_g[l], ln_b[l])
    return h.astype(out_dtype)


def setup_inputs(seed: int = 0) -> dict:
    key = jax.random.key(seed)
    ks = iter(jax.random.split(key, 40))
    nrm = lambda shape, std: std * jax.random.normal(next(ks), shape, F32)
    uni = lambda shape, lo, hi: jax.random.uniform(next(ks), shape, F32, lo, hi)
    xavier = lambda fi, fo: BETA * math.sqrt(2.0 / (fi + fo))
    idx = jnp.arange(RW_WIDTH, dtype=F32) / (RW_WIDTH - 1)
    ratio = jnp.arange(DEPTH, dtype=F32) / max(DEPTH - 1, 1)
    speed = -7.0 + 5.0 * idx[None, :] ** (0.85 + ratio[:, None] ** 0.5)
    x_prompt = nrm((BATCH, SEQ, D_MODEL), 1.0)
    x_sample = nrm((DEC_BATCH, DEC_SEQ, D_MODEL), 1.0)
    ln_in_g = 1.0 + nrm((D_MODEL,), 0.02)
    ln_in_b = nrm((D_MODEL,), 0.02)
    w_in = nrm((DEPTH, D_MODEL, N_IN), D_MODEL ** -0.5)
    mu_rkv = uni((DEPTH, 3, RW_WIDTH), 0.0, 1.0)
    mu_wa = uni((DEPTH, 2, D_MODEL), 0.0, 1.0)
    w0 = (speed + 0.5)[:, None, :] + nrm((DEPTH, N_DIR, RW_WIDTH), 0.1)
    w1 = nrm((DEPTH, N_DIR, D_MODEL, R_DECAY), D_MODEL ** -0.5)
    w2 = nrm((DEPTH, N_DIR, R_DECAY, RW_WIDTH), 0.1 * R_DECAY ** -0.5)
    a0 = nrm((DEPTH, N_DIR, RW_WIDTH), 0.1)
    a1 = nrm((DEPTH, N_DIR, D_MODEL, R_AAA), D_MODEL ** -0.5)
    a2 = nrm((DEPTH, N_DIR, R_AAA, RW_WIDTH), 0.1 * R_AAA ** -0.5)
    mu_v = uni((DEPTH - 1, D_MODEL), 0.0, 1.0)
    v0 = 1.0 + nrm((DEPTH - 1, RW_WIDTH), 0.1)
    v1 = nrm((DEPTH - 1, D_MODEL, R_MV), D_MODEL ** -0.5)
    v2 = nrm((DEPTH - 1, R_MV, RW_WIDTH), 0.1 * R_MV ** -0.5)
    k_k = 0.85 + nrm((DEPTH, RW_WIDTH), 0.02)
    k_a = 1.0 + nrm((DEPTH, RW_WIDTH), 0.02)
    r_k = -0.04 + nrm((DEPTH, RW_HEADS, RW_HEAD_SIZE), 0.02)
    lnx_g = 1.0 + nrm((DEPTH, RW_WIDTH), 0.02)
    lnx_b = nrm((DEPTH, RW_WIDTH), 0.02)
    p_rw = nrm((DEPTH, RW_WIDTH, D_MODEL), xavier(RW_WIDTH, D_MODEL))
    conv_w = nrm((DEPTH, CONV_W, MB_CONV_CH), CONV_W ** -0.5)
    conv_b = nrm((DEPTH, MB_CONV_CH), 0.02)
    u = uni((DEPTH, N_DIR, MB_HEADS), 0.0, 1.0)
    dt0 = jnp.exp(u * (math.log(0.1) - math.log(0.001)) + math.log(0.001))
    dt_bias = dt0 + jnp.log(-jnp.expm1(-dt0))
    a_log = jnp.log(uni((DEPTH, N_DIR, MB_HEADS), 1.0, 16.0))
    d_skip = 1.0 + nrm((DEPTH, MB_HEADS), 0.1)
    mb_norm_g = 1.0 + nrm((DEPTH, MB_WIDTH), 0.02)
    p_mb = nrm((DEPTH, MB_WIDTH, D_MODEL), xavier(MB_WIDTH, D_MODEL))
    w_out = nrm((DEPTH, D_MODEL, D_MODEL), xavier(D_MODEL, D_MODEL))
    ln_g = 1.0 + nrm((DEPTH, D_MODEL), 0.02)
    ln_b = nrm((DEPTH, D_MODEL), 0.02)
    return {'x_prompt': x_prompt, 'x_sample': x_sample, 'ln_in_g': ln_in_g, 'ln_in_b': ln_in_b,
            'w_in': w_in, 'mu_rkv': mu_rkv, 'mu_wa': mu_wa, 'w0': w0, 'w1': w1, 'w2': w2,
            'a0': a0, 'a1': a1, 'a2': a2, 'mu_v': mu_v, 'v0': v0, 'v1': v1, 'v2': v2,
            'k_k': k_k, 'k_a': k_a, 'r_k': r_k, 'lnx_g': lnx_g, 'lnx_b': lnx_b, 'p_rw': p_rw,
            'conv_w': conv_w, 'conv_b': conv_b, 'dt_bias': dt_bias, 'a_log': a_log, 'd_skip': d_skip,
            'mb_norm_g': mb_norm_g, 'p_mb': p_mb, 'w_out': w_out, 'ln_g': ln_g, 'ln_b': ln_b}


def reference(x_prompt, x_sample, ln_in_g, ln_in_b, w_in, mu_rkv, mu_wa, w0, w1, w2, a0, a1, a2,
              mu_v, v0, v1, v2, k_k, k_a, r_k, lnx_g, lnx_b, p_rw, conv_w, conv_b, dt_bias, a_log,
              d_skip, mb_norm_g, p_mb, w_out, ln_g, ln_b):
    y_prompt = trunk(x_prompt, ln_in_g, ln_in_b, w_in, mu_rkv, mu_wa, w0, w1, w2, a0, a1, a2, mu_v, v0, v1, v2,
                     k_k, k_a, r_k, lnx_g, lnx_b, p_rw, conv_w, conv_b, dt_bias, a_log, d_skip, mb_norm_g,
                     p_mb, w_out, ln_g, ln_b)
    y_sample = trunk(x_sample, ln_in_g, ln_in_b, w_in, mu_rkv, mu_wa, w0, w1, w2, a0, a1, a2, mu_v, v0, v1, v2,
                     k_k, k_a, r_k, lnx_g, lnx_b, p_rw, conv_w, conv_b, dt_bias, a_log, d_skip, mb_norm_g,
                     p_mb, w_out, ln_g, ln_b)
    return (y_prompt, y_sample)
```

```python
import functools
import math

import jax
import jax.numpy as jnp
from jax import lax
from jax.experimental import pallas as pl
from jax.experimental.pallas import tpu as pltpu

F32 = jnp.float32
BF16 = jnp.bfloat16

DEPTH = 2
RW_HEAD = 64
RW_GN_EPS = 64e-5
MB_HEAD = 64
MB_GROUPS = 8
MB_STATE = 128
CONV_W = 7
ALPHA = (2 * DEPTH) ** 0.25
LN_EPS = 1e-5
RMS_EPS = 1e-5

LANES = 128
RW_CHUNK = 64
SSD_CHUNK = 128
VMEM_LIMIT = 56 * 1024 * 1024


def _mm_kernel(x_ref, w_ref, o_ref):
    o_ref[...] = jnp.dot(x_ref[...], w_ref[...],
                         preferred_element_type=F32).astype(o_ref.dtype)


def _matmul(x, w, *, tm=1024, tn=512, out_dtype=F32):
    m, k = x.shape
    n = w.shape[1]
    tm = min(tm, m)
    tn = min(tn, n)
    assert m % tm == 0 and n % tn == 0
    return pl.pallas_call(
        _mm_kernel,
        grid=(m // tm, n // tn),
        in_specs=[pl.BlockSpec((tm, k), lambda i, j: (i, 0)),
                  pl.BlockSpec((k, tn), lambda i, j: (0, j))],
        out_specs=pl.BlockSpec((tm, tn), lambda i, j: (i, j)),
        out_shape=jax.ShapeDtypeStruct((m, n), out_dtype),
        compiler_params=pltpu.CompilerParams(
            dimension_semantics=("parallel", "parallel"),
            vmem_limit_bytes=VMEM_LIMIT),
        name="dense_matmul",
    )(x, w)


def _dotf(a, b, exact=False):
    if exact:
        return jnp.dot(a, b, preferred_element_type=F32,
                       precision=lax.Precision.HIGHEST)
    return jnp.dot(a.astype(BF16), b.astype(BF16), preferred_element_type=F32)


def _dot_nt(a, b):
    return lax.dot_general(a.astype(BF16), b.astype(BF16),
                           (((1,), (1,)), ((), ())), preferred_element_type=F32)


def _rwkv_chunk(refs, ka, y_ref, h_ref, row0, reverse):
    r_ref, k_ref, v_ref, kk_ref, lw_ref, a_ref = refs
    c = RW_CHUNK
    sl = (0, pl.ds(row0, c), slice(None))
    r = r_ref[sl]
    k = k_ref[sl]
    v = v_ref[sl]
    kk = kk_ref[sl]
    lw = lw_ref[sl]
    a = a_ref[sl]

    row = lax.broadcasted_iota(jnp.int32, (LANES, LANES), 0)
    col = lax.broadcasted_iota(jnp.int32, (LANES, LANES), 1)
    same_head = (row >= c) == (col >= c)
    tl = row % c
    ts = col % c
    if reverse:
        tl, ts = ts, tl
    strict = same_head & (tl > ts)
    incl = same_head & (tl >= ts)

    ci = lax.broadcasted_iota(jnp.int32, (c, c), 0)
    cj = lax.broadcasted_iota(jnp.int32, (c, c), 1)
    cum_m = ((cj >= ci) if reverse else (cj <= ci)).astype(F32)
    cl = _dotf(cum_m, lw, exact=True)
    ce = cl - lw
    ct = cl[0:1, :] if reverse else cl[c - 1:c, :]

    kd = k * (1.0 + (a - 1.0) * ka)
    bv = kk * a
    einv = jnp.exp(-cl)
    dec_end = jnp.exp(ct - cl)

    def bd(x):
        return jnp.where(same_head, jnp.concatenate([x, x], axis=0), 0.0)

    xa = bd(-kk * jnp.exp(ce))
    xr = bd(r * jnp.exp(cl))
    yb = bd(bv * einv)
    yk = bd(kd * einv)
    vb = bd(v)
    bh = bd(bv * dec_end)
    kh = bd(kd * dec_end)

    lhs = jnp.concatenate([xa, xr], axis=0)
    gram = _dot_nt(lhs, jnp.concatenate([yb, yk], axis=0))
    a_ab = jnp.where(strict, gram[:LANES, :LANES], 0.0)
    a_ak = jnp.where(strict, gram[:LANES, LANES:], 0.0)
    a_rb = jnp.where(incl, gram[LANES:, :LANES], 0.0)
    a_rk = jnp.where(incl, gram[LANES:, LANES:], 0.0)

    eye = (row == col).astype(F32)
    inv = eye + jnp.where((tl // 2) == (ts // 2), a_ab, 0.0)
    s = 2
    while s < c:
        off = jnp.where(((tl // (2 * s)) == (ts // (2 * s))) & ((tl // s) != (ts // s)),
                        a_ab, 0.0)
        inv = inv + _dotf(_dotf(inv, off), inv)
        s *= 2

    h = h_ref[...]
    xh = _dot_nt(lhs, h)
    u = _dotf(inv, xh[:LANES] + _dotf(a_ak, vb))
    uv = jnp.concatenate([u, vb], axis=0)
    y = xh[LANES:] + _dotf(jnp.concatenate([a_rb, a_rk], axis=1), uv)
    y_ref[sl] = y[:c] + y[c:]

    bk = jnp.concatenate([bh, kh], axis=0)
    h_ref[...] = h * jnp.exp(ct) + lax.dot_general(
        uv.astype(BF16), bk.astype(BF16), (((0,), (0,)), ((), ())),
        preferred_element_type=F32)


def _rwkv_kernel(ka_ref, *refs, n_chunks):
    fwd, bwd = refs[0:6], refs[6:12]
    yf_ref, yb_ref, hf_ref, hb_ref = refs[12:16]

    @pl.when(pl.program_id(2) == 0)
    def _():
        hf_ref[...] = jnp.zeros_like(hf_ref)
        hb_ref[...] = jnp.zeros_like(hb_ref)

    ka = ka_ref[...]

    def body(j, carry):
        _rwkv_chunk(fwd, ka, yf_ref, hf_ref, pl.multiple_of(j * RW_CHUNK, RW_CHUNK), False)
        jb = n_chunks - 1 - j
        _rwkv_chunk(bwd, ka, yb_ref, hb_ref, pl.multiple_of(jb * RW_CHUNK, RW_CHUNK), True)
        return carry

    lax.fori_loop(0, n_chunks, body, 0)


def _rwkv_scan(r, k, v, kk, lw, a, k_a, *, tb=512):
    bsz, t, w = r.shape
    tb = min(tb, t)
    nb = t // tb
    assert t % tb == 0 and tb % RW_CHUNK == 0 and w % LANES == 0
    blk = (1, tb, LANES)
    f_map = lambda p, b, i: (b, i, p)
    b_map = lambda p, b, i: (b, nb - 1 - i, p)
    spec_f = pl.BlockSpec(blk, f_map)
    spec_b = pl.BlockSpec(blk, b_map)
    out = jax.ShapeDtypeStruct((bsz, t, w), F32)
    return pl.pallas_call(
        functools.partial(_rwkv_kernel, n_chunks=tb // RW_CHUNK),
        grid=(w // LANES, bsz, nb),
        in_specs=[pl.BlockSpec((1, LANES), lambda p, b, i: (0, p))]
                 + [spec_f] * 6 + [spec_b] * 6,
        out_specs=[spec_f, spec_b],
        out_shape=[out, out],
        scratch_shapes=[pltpu.VMEM((LANES, LANES), F32), pltpu.VMEM((LANES, LANES), F32)],
        compiler_params=pltpu.CompilerParams(
            dimension_semantics=("parallel", "parallel", "arbitrary"),
            vmem_limit_bytes=VMEM_LIMIT),
        name="rwkv7_scan",
    )(k_a.reshape(1, w), r, k, v, kk, lw[0], a[0], r, k, v, kk, lw[1], a[1])


def _ssd_chunk(refs, y_ref, s_ref, row0, reverse):
    dtm_ref, adtm_ref, adth_ref, x_ref, b_ref, c_ref = refs
    c = SSD_CHUNK
    rows = pl.ds(row0, c)
    dt_tm = dtm_ref[0, 0, 0, rows, :]
    adt_tm = adtm_ref[0, 0, 0, rows, :]
    adt_hm = adth_ref[0, 0, 0, :, rows]
    bc = b_ref[0, rows, :]
    cc = c_ref[0, rows, :]

    li = lax.broadcasted_iota(jnp.int32, (c, c), 0)
    si = lax.broadcasted_iota(jnp.int32, (c, c), 1)
    mask = (si >= li) if reverse else (si <= li)
    cum_m = mask.astype(F32)
    cum_t = ((li >= si) if reverse else (li <= si)).astype(F32)
    acum_tm = _dotf(cum_m, adt_tm, exact=True)
    acum_hm = _dotf(adt_hm, cum_t, exact=True)
    tot = acum_hm[:, 0:1] if reverse else acum_hm[:, c - 1:c]

    cb = _dot_nt(cc, bc)
    bt = bc.T
    lane = lax.broadcasted_iota(jnp.int32, (c, LANES), 1)
    first = lane < MB_HEAD
    lane1 = lax.broadcasted_iota(jnp.int32, (1, LANES), 1) < MB_HEAD

    n_pairs = x_ref.shape[2] // LANES
    for q in range(n_pairs):
        e1, e2 = 2 * q, 2 * q + 1
        col1 = jnp.broadcast_to(acum_tm[:, e1:e1 + 1], (c, LANES))
        col2 = jnp.broadcast_to(acum_tm[:, e2:e2 + 1], (c, LANES))
        colp = jnp.where(first, col1, col2)
        dtp = jnp.where(first, jnp.broadcast_to(dt_tm[:, e1:e1 + 1], (c, LANES)),
                        jnp.broadcast_to(dt_tm[:, e2:e2 + 1], (c, LANES)))
        totp = jnp.where(lane1, jnp.broadcast_to(tot[e1:e1 + 1, :], (1, LANES)),
                         jnp.broadcast_to(tot[e2:e2 + 1, :], (1, LANES)))
        xp = x_ref[0, rows, q * LANES:(q + 1) * LANES] * dtp
        l1 = jnp.where(mask, jnp.exp(jnp.minimum(col1 - acum_hm[e1:e1 + 1, :], 0.0)), 0.0)
        l2 = jnp.where(mask, jnp.exp(jnp.minimum(col2 - acum_hm[e2:e2 + 1, :], 0.0)), 0.0)
        yd = jnp.where(first, _dotf(l1 * cb, xp), _dotf(l2 * cb, xp))
        st = s_ref[q]
        yo = jnp.exp(colp) * _dotf(cc, st)
        y_ref[0, rows, q * LANES:(q + 1) * LANES] = yd + yo
        s_ref[q] = jnp.exp(totp) * st + _dotf(bt, jnp.exp(totp - colp) * xp)


def _ssd_kernel(*refs, n_chunks):
    fwd, bwd = refs[0:6], refs[6:12]
    yf_ref, yb_ref, sf_ref, sb_ref = refs[12:16]

    @pl.when(pl.program_id(2) == 0)
    def _():
        sf_ref[...] = jnp.zeros_like(sf_ref)
        sb_ref[...] = jnp.zeros_like(sb_ref)

    def body(j, carry):
        _ssd_chunk(fwd, yf_ref, sf_ref, pl.multiple_of(j * SSD_CHUNK, SSD_CHUNK), False)
        jb = n_chunks - 1 - j
        _ssd_chunk(bwd, yb_ref, sb_ref, pl.multiple_of(jb * SSD_CHUNK, SSD_CHUNK), True)
        return carry

    lax.fori_loop(0, n_chunks, body, 0)


def _ssd_scan(xbc, dt, a_neg, *, tb=512):
    bsz, t, _ = xbc.shape
    g = MB_GROUPS
    nh = dt.shape[-1]
    e = nh // g
    w = nh * MB_HEAD
    gw = e * MB_HEAD
    tb = min(tb, t)
    nb = t // tb
    assert t % tb == 0 and tb % SSD_CHUNK == 0 and MB_STATE == LANES
    dt_g = dt.reshape(bsz, t, 2, g, e)
    adt_g = dt_g * a_neg.reshape(2, g, e)
    dt_tm = jnp.transpose(dt_g, (2, 0, 3, 1, 4))
    adt_tm = jnp.transpose(adt_g, (2, 0, 3, 1, 4))
    adt_hm = jnp.transpose(adt_g, (2, 0, 3, 4, 1))

    def specs(d, tmap):
        return [
            pl.BlockSpec((1, 1, 1, tb, e), lambda b, gi, i: (d, b, gi, tmap(i), 0)),
            pl.BlockSpec((1, 1, 1, tb, e), lambda b, gi, i: (d, b, gi, tmap(i), 0)),
            pl.BlockSpec((1, 1, 1, e, tb), lambda b, gi, i: (d, b, gi, 0, tmap(i))),
            pl.BlockSpec((1, tb, gw), lambda b, gi, i: (b, tmap(i), gi)),
            pl.BlockSpec((1, tb, MB_STATE), lambda b, gi, i: (b, tmap(i), w // MB_STATE + gi)),
            pl.BlockSpec((1, tb, MB_STATE), lambda b, gi, i: (b, tmap(i), w // MB_STATE + g + gi)),
        ]

    fmap = lambda i: i
    bmap = lambda i: nb - 1 - i
    out = jax.ShapeDtypeStruct((bsz, t, w), F32)
    n_pairs = gw // LANES
    return pl.pallas_call(
        functools.partial(_ssd_kernel, n_chunks=tb // SSD_CHUNK),
        grid=(bsz, g, nb),
        in_specs=specs(0, fmap) + specs(1, bmap),
        out_specs=[pl.BlockSpec((1, tb, gw), lambda b, gi, i: (b, i, gi)),
                   pl.BlockSpec((1, tb, gw), lambda b, gi, i: (b, nb - 1 - i, gi))],
        out_shape=[out, out],
        scratch_shapes=[pltpu.VMEM((n_pairs, MB_STATE, LANES), F32),
                        pltpu.VMEM((n_pairs, MB_STATE, LANES), F32)],
        compiler_params=pltpu.CompilerParams(
            dimension_semantics=("parallel", "parallel", "arbitrary"),
            vmem_limit_bytes=VMEM_LIMIT),
        name="ssd_scan",
    )(dt_tm, adt_tm, adt_hm, xbc, xbc, xbc, dt_tm, adt_tm, adt_hm, xbc, xbc, xbc)


def _layer_norm(x, g, b):
    mu = jnp.mean(x, -1, keepdims=True)
    xc = x - mu
    var = jnp.mean(xc * xc, -1, keepdims=True)
    return xc * lax.rsqrt(var + LN_EPS) * g + b


def _centred_delta(t):
    tp = jnp.pad(t, ((0, 0), (1, 1), (0, 0)))
    return 0.5 * (tp[:, :-2] + tp[:, 2:]) - t


def _pad_cols(wm, n):
    return jnp.pad(wm, ((0, 0), (0, n - wm.shape[1])))


def _pad_rows(wm, n):
    return jnp.pad(wm, ((0, n - wm.shape[0]), (0, 0)))


def _lora(x, w_a, w_b, act=None):
    bsz, t, d = x.shape
    xm = x.reshape(bsz * t, d).astype(BF16)
    wa = jnp.concatenate([_pad_cols(m, LANES) for m in w_a], axis=1).astype(BF16)
    mid = _matmul(xm, wa, tn=wa.shape[1])
    if act is not None:
        mid = act(mid)
    mid = mid.astype(BF16)
    outs = []
    for i, m in enumerate(w_b):
        wb = _pad_rows(m, LANES).astype(BF16)
        outs.append(_matmul(mid[:, i * LANES:(i + 1) * LANES], wb).reshape(bsz, t, -1))
    return outs


def _trunk(x, ln_in_g, ln_in_b, w_in, mu_rkv, mu_wa, w0, w1, w2, a0, a1, a2, mu_v, v0, v1, v2,
           k_k, k_a, r_k, lnx_g, lnx_b, p_rw, conv_w, conv_b, dt_bias, a_log, d_skip, mb_norm_g,
           p_mb, w_out, ln_g, ln_b):
    bsz, t, d = x.shape
    m = bsz * t
    rw_w = d
    mb_w = 2 * d
    rw_heads = rw_w // RW_HEAD
    mb_heads = mb_w // MB_HEAD
    conv_ch = mb_w + 2 * MB_GROUPS * MB_STATE
    splits = [3 * rw_w, 4 * rw_w, 4 * rw_w + mb_w, 4 * rw_w + mb_w + conv_ch,
              4 * rw_w + mb_w + conv_ch + 2 * mb_heads]
    hs = (bsz, t, rw_heads, RW_HEAD)

    h = _layer_norm(x, ln_in_g, ln_in_b)
    v_first = None
    for l in range(DEPTH):
        hb = h.reshape(m, d).astype(BF16)
        wl = w_in[l].astype(BF16)
        bounds = [0] + splits + [wl.shape[1]]
        p_rkv, g_rw, z, xbc, dt_raw, gates = [
            _matmul(hb, wl[:, bounds[i]:bounds[i + 1]]).reshape(bsz, t, -1) for i in range(6)]

        p_rkv = p_rkv + mu_rkv[l].reshape(-1) * _centred_delta(p_rkv)
        r, k, v = jnp.split(p_rkv, 3, axis=-1)
        hd = _centred_delta(h)
        xw = h + hd * mu_wa[l, 0]
        xa = h + hd * mu_wa[l, 1]
        w_lora = jnp.stack(_lora(xw, [w1[l, 0], w1[l, 1]], [w2[l, 0], w2[l, 1]], act=jnp.tanh))
        w_log = -jax.nn.softplus(-(w0[l][:, None, None, :] + w_lora)) - 0.5
        lw = -jnp.exp(w_log)
        a_sig = jax.nn.sigmoid(a0[l][:, None, None, :]
                               + jnp.stack(_lora(xa, [a1[l, 0], a1[l, 1]], [a2[l, 0], a2[l, 1]])))
        if l == 0:
            v_first = v
        else:
            xv = h + hd * mu_v[l - 1]
            vg = jax.nn.sigmoid(v0[l - 1] + _lora(xv, [v1[l - 1]], [v2[l - 1]])[0])
            v = v + (v_first - v) * vg
        kk = (k * k_k[l]).reshape(hs)
        kk = kk / jnp.maximum(jnp.sqrt(jnp.sum(kk * kk, -1, keepdims=True)), 1e-12)
        kk = kk.reshape(bsz, t, rw_w)
        y_f, y_b = _rwkv_scan(r, k, v, kk, lw, a_sig, k_a[l])
        y = (y_f + y_b).reshape(hs)
        mu = jnp.mean(y, -1, keepdims=True)
        yc = y - mu
        var = jnp.mean(yc * yc, -1, keepdims=True)
        gn = (yc * lax.rsqrt(var + RW_GN_EPS)).reshape(bsz, t, rw_w) * lnx_g[l] + lnx_b[l]
        k_dir = k[None] * (1.0 + (a_sig - 1.0) * k_a[l])
        bonus = jnp.sum((r[None] * k_dir).reshape((2,) + hs) * r_k[l], axis=(0, -1))[..., None] * v.reshape(hs)
        o_rw = (gn + bonus.reshape(bsz, t, rw_w)) * jax.nn.silu(g_rw)

        xp = jnp.pad(xbc, ((0, 0), (CONV_W // 2, CONV_W // 2), (0, 0)))
        conv = conv_b[l] + sum(xp[:, j:j + t] * conv_w[l, j] for j in range(CONV_W))
        xbc_c = jax.nn.silu(conv)
        dt = jax.nn.softplus(dt_raw.reshape(bsz, t, 2, mb_heads) + dt_bias[l])
        a_neg = -jnp.exp(a_log[l])
        ys_f, ys_b = _ssd_scan(xbc_c, dt, a_neg)
        xm = xbc_c[..., :mb_w]
        skip = (d_skip[l][:, None] * xm.reshape(bsz, t, mb_heads, MB_HEAD)).reshape(bsz, t, mb_w)
        ym = (ys_f + ys_b + skip) * jax.nn.silu(z)
        yg = ym.reshape(bsz, t, MB_GROUPS, mb_w // MB_GROUPS)
        yg = yg * lax.rsqrt(jnp.mean(yg * yg, -1, keepdims=True) + RMS_EPS)
        o_mb = yg.reshape(bsz, t, mb_w) * mb_norm_g[l]

        g_a, g_b = jnp.split(gates, 2, axis=-1)
        pr = _matmul(o_rw.reshape(m, rw_w).astype(BF16), p_rw[l].astype(BF16)).reshape(bsz, t, d)
        pm = _matmul(o_mb.reshape(m, mb_w).astype(BF16), p_mb[l].astype(BF16), tm=512).reshape(bsz, t, d)
        mix = jax.nn.sigmoid(g_a) * pr + jax.nn.sigmoid(g_b) * pm
        out = _matmul(mix.reshape(m, d).astype(BF16), w_out[l].astype(BF16)).reshape(bsz, t, d)
        h = _layer_norm(ALPHA * h + out, ln_g[l], ln_b[l])
    return h


def kernel(x_prompt, x_sample, ln_in_g, ln_in_b, w_in, mu_rkv, mu_wa, w0, w1, w2, a0, a1, a2, mu_v, v0, v1, v2, k_k, k_a, r_k, lnx_g, lnx_b, p_rw, conv_w, conv_b, dt_bias, a_log, d_skip, mb_norm_g, p_mb, w_out, ln_g, ln_b):
    assert x_prompt.shape[1:] == x_sample.shape[1:]
    nb = x_prompt.shape[0]
    x = jnp.concatenate([x_prompt, x_sample], axis=0)
    y = _trunk(x, ln_in_g, ln_in_b, w_in, mu_rkv, mu_wa, w0, w1, w2, a0, a1, a2, mu_v, v0, v1, v2,
               k_k, k_a, r_k, lnx_g, lnx_b, p_rw, conv_w, conv_b, dt_bias, a_log, d_skip, mb_norm_g,
               p_mb, w_out, ln_g, ln_b)
    y = y.astype(x_prompt.dtype)
    return (y[:nb], y[nb:])
```

```python
import functools
import math

import numpy as np
import jax
import jax.numpy as jnp
from jax import lax
from jax.experimental import pallas as pl
from jax.experimental.pallas import tpu as pltpu

F32 = jnp.float32
BF16 = jnp.bfloat16

DEPTH = 2
RW_HEAD = 64
RW_GN_EPS = 64e-5
MB_HEAD = 64
MB_GROUPS = 8
MB_STATE = 128
CONV_W = 7
ALPHA = (2 * DEPTH) ** 0.25
LN_EPS = 1e-5
RMS_EPS = 1e-5

LANES = 128
RW_CHUNK = 64
SSD_CHUNK = 128
VMEM_LIMIT = 56 * 1024 * 1024


def _mm_kernel(x_ref, w_ref, o_ref):
    o_ref[...] = jnp.dot(x_ref[...], w_ref[...],
                         preferred_element_type=F32).astype(o_ref.dtype)


def _matmul(x, w, *, tm=1024, tn=512, out_dtype=F32):
    m, k = x.shape
    n = w.shape[1]
    tm = min(tm, m)
    tn = min(tn, n)
    assert m % tm == 0 and n % tn == 0
    return pl.pallas_call(
        _mm_kernel,
        grid=(m // tm, n // tn),
        in_specs=[pl.BlockSpec((tm, k), lambda i, j: (i, 0)),
                  pl.BlockSpec((k, tn), lambda i, j: (0, j))],
        out_specs=pl.BlockSpec((tm, tn), lambda i, j: (i, j)),
        out_shape=jax.ShapeDtypeStruct((m, n), out_dtype),
        compiler_params=pltpu.CompilerParams(
            dimension_semantics=("parallel", "parallel"),
            vmem_limit_bytes=VMEM_LIMIT),
        name="dense_matmul",
    )(x, w)


def _dotf(a, b, exact=False):
    if exact:
        return jnp.dot(a, b, preferred_element_type=F32,
                       precision=lax.Precision.HIGHEST)
    return jnp.dot(a.astype(BF16), b.astype(BF16), preferred_element_type=F32)


def _dot_nt(a, b):
    return lax.dot_general(a.astype(BF16), b.astype(BF16),
                           (((1,), (1,)), ((), ())), preferred_element_type=F32)


def _dot_tn(a, b):
    return lax.dot_general(a.astype(BF16), b.astype(BF16),
                           (((0,), (0,)), ((), ())), preferred_element_type=F32)


_M_SAME, _M_EYE, _M_STRICT, _M_INCL, _M_BASE, _M_LEVEL = 0, 1, 2, 3, 6, 7


def _rwkv_masks():
    c = RW_CHUNK
    i = np.arange(LANES)
    row, col = i[:, None], i[None, :]
    same = (row >= c) == (col >= c)
    tl, ts = row % c, col % c
    out = [same, row == col,
           same & (tl > ts), same & (tl >= ts),
           same & (tl < ts), same & (tl <= ts),
           same & ((tl // 2) == (ts // 2))]
    s = 2
    while s < c:
        out.append(same & ((tl // (2 * s)) == (ts // (2 * s))) & ((tl // s) != (ts // s)))
        s *= 2
    return np.stack(out).astype(np.float32)


def _rwkv_cum():
    i = np.arange(RW_CHUNK)
    return np.stack([i[None, :] <= i[:, None], i[None, :] >= i[:, None]]).astype(np.float32)


def _rwkv_local(ci, refs, row0, reverse, ka, msk_ref, cum_ref, scr):
    mw_s, n_s, q_s, yl_s, dec_s = scr
    r_ref, k_ref, v_ref, kk_ref, lw_ref, a_ref = refs
    c = RW_CHUNK
    d = 1 if reverse else 0
    sl = (0, pl.ds(row0, c), slice(None))
    r = r_ref[sl]
    k = k_ref[sl]
    v = v_ref[sl]
    kk = kk_ref[sl]
    lw = lw_ref[sl]
    a = a_ref[sl]
    same = msk_ref[_M_SAME]
    strict = msk_ref[_M_STRICT + 2 * d]
    incl = msk_ref[_M_INCL + 2 * d]

    cl = _dotf(cum_ref[d], lw, exact=True)
    yield
    ce = cl - lw
    ct = cl[0:1, :] if reverse else cl[c - 1:c, :]
    kd = k * (1.0 + (a - 1.0) * ka)
    bv = kk * a
    einv = jnp.exp(-cl)
    dec_end = jnp.exp(ct - cl)
    dec_s[ci] = jnp.exp(ct)

    def bd(x):
        return jnp.concatenate([x, x], axis=0) * same

    xa = bd(-kk * jnp.exp(ce)).astype(BF16)
    xr = bd(r * jnp.exp(cl))
    vb = bd(v).astype(BF16)
    bh = bd(bv * dec_end).astype(BF16)
    kh = bd(kd * dec_end).astype(BF16)

    gram = _dot_nt(jnp.concatenate([xa, xr.astype(BF16)], axis=0),
                   jnp.concatenate([bd(bv * einv), bd(kd * einv)], axis=0))
    yield
    a_ab = gram[:LANES, :LANES] * strict
    a_ak = gram[:LANES, LANES:] * strict
    a_rb = (gram[LANES:, :LANES] * incl).astype(BF16)
    a_rk = (gram[LANES:, LANES:] * incl).astype(BF16)

    inv = msk_ref[_M_EYE] + a_ab * msk_ref[_M_BASE]
    g = _dotf(a_ak, vb)
    yk = _dotf(a_rk, vb)
    for lvl in range(int(math.log2(c)) - 1):
        off = (a_ab * msk_ref[_M_LEVEL + lvl]).astype(BF16)
        ib = inv.astype(BF16)
        t = _dotf(ib, off)
        yield
        t = _dotf(t, ib)
        yield
        inv = inv + t

    wu = _dotf(inv, jnp.concatenate([xa, g.astype(BF16)], axis=1))
    yield
    wu = wu.astype(BF16)
    mw = _dot_tn(wu[:, :LANES], bh)
    nn = _dot_tn(jnp.concatenate([wu[:, LANES:], vb], axis=0),
                 jnp.concatenate([bh, kh], axis=0))
    qy = _dotf(a_rb, wu)
    yield
    mw_s[ci] = mw.astype(BF16)
    n_s[ci] = nn
    q_s[ci] = (xr + qy[:, :LANES]).astype(BF16)
    yl = qy[:, LANES:] + yk
    yl_s[ci] = yl[:c] + yl[c:]


def _rwkv_state_step(ci, y_ref, row0, h_ref, scr):
    mw_s, n_s, q_s, yl_s, dec_s = scr
    c = RW_CHUNK
    h = h_ref[...]
    hb = h.astype(BF16)
    y = _dot_nt(q_s[ci], hb)
    y_ref[0, pl.ds(row0, c), :] = y[:c] + y[c:] + yl_s[ci]
    h_ref[...] = h * dec_s[ci] + _dotf(hb, mw_s[ci]) + n_s[ci]


def _rwkv_kernel(ka_ref, msk_ref, cum_ref, *refs, n_chunks):
    fwd, bwd = refs[0:6], refs[6:12]
    yf_ref, yb_ref, hf_ref, hb_ref = refs[12:16]
    scr = refs[16:21]

    @pl.when(pl.program_id(2) == 0)
    def _():
        hf_ref[...] = jnp.zeros_like(hf_ref)
        hb_ref[...] = jnp.zeros_like(hb_ref)

    ka = ka_ref[...]
    stages = []
    for j in range(n_chunks):
        stages.append(_rwkv_local(j, fwd, j * RW_CHUNK, False, ka, msk_ref, cum_ref, scr))
        stages.append(_rwkv_local(n_chunks + j, bwd, j * RW_CHUNK, True, ka, msk_ref, cum_ref, scr))
    while stages:
        live = []
        for gen in stages:
            if next(gen, StopIteration) is not StopIteration:
                live.append(gen)
        stages = live
    for j in range(n_chunks):
        _rwkv_state_step(j, yf_ref, j * RW_CHUNK, hf_ref, scr)
        jb = n_chunks - 1 - j
        _rwkv_state_step(n_chunks + jb, yb_ref, jb * RW_CHUNK, hb_ref, scr)


def _rwkv_scan(r, k, v, kk, lw, a, k_a, *, tb=512):
    bsz, t, w = r.shape
    tb = min(tb, t)
    nb = t // tb
    nch = tb // RW_CHUNK
    assert t % tb == 0 and tb % RW_CHUNK == 0 and w % LANES == 0
    blk = (1, tb, LANES)
    f_map = lambda p, b, i: (b, i, p)
    b_map = lambda p, b, i: (b, nb - 1 - i, p)
    spec_f = pl.BlockSpec(blk, f_map)
    spec_b = pl.BlockSpec(blk, b_map)
    msk = jnp.asarray(_rwkv_masks())
    cum = jnp.asarray(_rwkv_cum())
    out = jax.ShapeDtypeStruct((bsz, t, w), F32)
    return pl.pallas_call(
        functools.partial(_rwkv_kernel, n_chunks=nch),
        grid=(w // LANES, bsz, nb),
        in_specs=[pl.BlockSpec((1, LANES), lambda p, b, i: (0, p)),
                  pl.BlockSpec(msk.shape, lambda p, b, i: (0, 0, 0)),
                  pl.BlockSpec(cum.shape, lambda p, b, i: (0, 0, 0))]
                 + [spec_f] * 6 + [spec_b] * 6,
        out_specs=[spec_f, spec_b],
        out_shape=[out, out],
        scratch_shapes=[pltpu.VMEM((LANES, LANES), F32), pltpu.VMEM((LANES, LANES), F32),
                        pltpu.VMEM((2 * nch, LANES, LANES), BF16),
                        pltpu.VMEM((2 * nch, LANES, LANES), F32),
                        pltpu.VMEM((2 * nch, LANES, LANES), BF16),
                        pltpu.VMEM((2 * nch, RW_CHUNK, LANES), F32),
                        pltpu.VMEM((2 * nch, 1, LANES), F32)],
        compiler_params=pltpu.CompilerParams(
            dimension_semantics=("parallel", "parallel", "arbitrary"),
            vmem_limit_bytes=VMEM_LIMIT),
        name="rwkv7_scan",
    )(k_a.reshape(1, w), msk, cum, r, k, v, kk, lw[0], a[0], r, k, v, kk, lw[1], a[1])


def _ssd_chunk(refs, y_ref, s_ref, row0, reverse):
    dtm_ref, adtm_ref, adth_ref, x_ref, b_ref, c_ref = refs
    c = SSD_CHUNK
    rows = pl.ds(row0, c)
    dt_tm = dtm_ref[0, 0, 0, rows, :]
    adt_tm = adtm_ref[0, 0, 0, rows, :]
    adt_hm = adth_ref[0, 0, 0, :, rows]
    bc = b_ref[0, rows, :]
    cc = c_ref[0, rows, :]

    li = lax.broadcasted_iota(jnp.int32, (c, c), 0)
    si = lax.broadcasted_iota(jnp.int32, (c, c), 1)
    mask = (si >= li) if reverse else (si <= li)
    cum_m = mask.astype(F32)
    cum_t = ((li >= si) if reverse else (li <= si)).astype(F32)
    acum_tm = _dotf(cum_m, adt_tm, exact=True)
    acum_hm = _dotf(adt_hm, cum_t, exact=True)
    tot = acum_hm[:, 0:1] if reverse else acum_hm[:, c - 1:c]

    cb = _dot_nt(cc, bc)
    bt = bc.T
    lane = lax.broadcasted_iota(jnp.int32, (c, LANES), 1)
    first = lane < MB_HEAD
    lane1 = lax.broadcasted_iota(jnp.int32, (1, LANES), 1) < MB_HEAD

    n_pairs = x_ref.shape[2] // LANES
    for q in range(n_pairs):
        e1, e2 = 2 * q, 2 * q + 1
        col1 = jnp.broadcast_to(acum_tm[:, e1:e1 + 1], (c, LANES))
        col2 = jnp.broadcast_to(acum_tm[:, e2:e2 + 1], (c, LANES))
        colp = jnp.where(first, col1, col2)
        dtp = jnp.where(first, jnp.broadcast_to(dt_tm[:, e1:e1 + 1], (c, LANES)),
                        jnp.broadcast_to(dt_tm[:, e2:e2 + 1], (c, LANES)))
        totp = jnp.where(lane1, jnp.broadcast_to(tot[e1:e1 + 1, :], (1, LANES)),
                         jnp.broadcast_to(tot[e2:e2 + 1, :], (1, LANES)))
        xp = x_ref[0, rows, q * LANES:(q + 1) * LANES] * dtp
        l1 = jnp.where(mask, jnp.exp(jnp.minimum(col1 - acum_hm[e1:e1 + 1, :], 0.0)), 0.0)
        l2 = jnp.where(mask, jnp.exp(jnp.minimum(col2 - acum_hm[e2:e2 + 1, :], 0.0)), 0.0)
        yd = jnp.where(first, _dotf(l1 * cb, xp), _dotf(l2 * cb, xp))
        st = s_ref[q]
        yo = jnp.exp(colp) * _dotf(cc, st)
        y_ref[0, rows, q * LANES:(q + 1) * LANES] = yd + yo
        s_ref[q] = jnp.exp(totp) * st + _dotf(bt, jnp.exp(totp - colp) * xp)


def _ssd_kernel(*refs, n_chunks):
    fwd, bwd = refs[0:6], refs[6:12]
    yf_ref, yb_ref, sf_ref, sb_ref = refs[12:16]

    @pl.when(pl.program_id(2) == 0)
    def _():
        sf_ref[...] = jnp.zeros_like(sf_ref)
        sb_ref[...] = jnp.zeros_like(sb_ref)

    def body(j, carry):
        _ssd_chunk(fwd, yf_ref, sf_ref, pl.multiple_of(j * SSD_CHUNK, SSD_CHUNK), False)
        jb = n_chunks - 1 - j
        _ssd_chunk(bwd, yb_ref, sb_ref, pl.multiple_of(jb * SSD_CHUNK, SSD_CHUNK), True)
        return carry

    lax.fori_loop(0, n_chunks, body, 0)


def _ssd_scan(xbc, dt, a_neg, *, tb=512):
    bsz, t, _ = xbc.shape
    g = MB_GROUPS
    nh = dt.shape[-1]
    e = nh // g
    w = nh * MB_HEAD
    gw = e * MB_HEAD
    tb = min(tb, t)
    nb = t // tb
    assert t % tb == 0 and tb % SSD_CHUNK == 0 and MB_STATE == LANES
    dt_g = dt.reshape(bsz, t, 2, g, e)
    adt_g = dt_g * a_neg.reshape(2, g, e)
    dt_tm = jnp.transpose(dt_g, (2, 0, 3, 1, 4))
    adt_tm = jnp.transpose(adt_g, (2, 0, 3, 1, 4))
    adt_hm = jnp.transpose(adt_g, (2, 0, 3, 4, 1))

    def specs(d, tmap):
        return [
            pl.BlockSpec((1, 1, 1, tb, e), lambda b, gi, i: (d, b, gi, tmap(i), 0)),
            pl.BlockSpec((1, 1, 1, tb, e), lambda b, gi, i: (d, b, gi, tmap(i), 0)),
            pl.BlockSpec((1, 1, 1, e, tb), lambda b, gi, i: (d, b, gi, 0, tmap(i))),
            pl.BlockSpec((1, tb, gw), lambda b, gi, i: (b, tmap(i), gi)),
            pl.BlockSpec((1, tb, MB_STATE), lambda b, gi, i: (b, tmap(i), w // MB_STATE + gi)),
            pl.BlockSpec((1, tb, MB_STATE), lambda b, gi, i: (b, tmap(i), w // MB_STATE + g + gi)),
        ]

    fmap = lambda i: i
    bmap = lambda i: nb - 1 - i
    out = jax.ShapeDtypeStruct((bsz, t, w), F32)
    n_pairs = gw // LANES
    return pl.pallas_call(
        functools.partial(_ssd_kernel, n_chunks=tb // SSD_CHUNK),
        grid=(bsz, g, nb),
        in_specs=specs(0, fmap) + specs(1, bmap),
        out_specs=[pl.BlockSpec((1, tb, gw), lambda b, gi, i: (b, i, gi)),
                   pl.BlockSpec((1, tb, gw), lambda b, gi, i: (b, nb - 1 - i, gi))],
        out_shape=[out, out],
        scratch_shapes=[pltpu.VMEM((n_pairs, MB_STATE, LANES), F32),
                        pltpu.VMEM((n_pairs, MB_STATE, LANES), F32)],
        compiler_params=pltpu.CompilerParams(
            dimension_semantics=("parallel", "parallel", "arbitrary"),
            vmem_limit_bytes=VMEM_LIMIT),
        name="ssd_scan",
    )(dt_tm, adt_tm, adt_hm, xbc, xbc, xbc, dt_tm, adt_tm, adt_hm, xbc, xbc, xbc)


def _layer_norm(x, g, b):
    mu = jnp.mean(x, -1, keepdims=True)
    xc = x - mu
    var = jnp.mean(xc * xc, -1, keepdims=True)
    return xc * lax.rsqrt(var + LN_EPS) * g + b


def _centred_delta(t):
    tp = jnp.pad(t, ((0, 0), (1, 1), (0, 0)))
    return 0.5 * (tp[:, :-2] + tp[:, 2:]) - t


def _pad_cols(wm, n):
    return jnp.pad(wm, ((0, 0), (0, n - wm.shape[1])))


def _pad_rows(wm, n):
    return jnp.pad(wm, ((0, n - wm.shape[0]), (0, 0)))


def _lora(x, w_a, w_b, act=None):
    bsz, t, d = x.shape
    xm = x.reshape(bsz * t, d).astype(BF16)
    wa = jnp.concatenate([_pad_cols(m, LANES) for m in w_a], axis=1).astype(BF16)
    mid = _matmul(xm, wa, tn=wa.shape[1])
    if act is not None:
        mid = act(mid)
    mid = mid.astype(BF16)
    outs = []
    for i, m in enumerate(w_b):
        wb = _pad_rows(m, LANES).astype(BF16)
        outs.append(_matmul(mid[:, i * LANES:(i + 1) * LANES], wb).reshape(bsz, t, -1))
    return outs


def _trunk(x, ln_in_g, ln_in_b, w_in, mu_rkv, mu_wa, w0, w1, w2, a0, a1, a2, mu_v, v0, v1, v2,
           k_k, k_a, r_k, lnx_g, lnx_b, p_rw, conv_w, conv_b, dt_bias, a_log, d_skip, mb_norm_g,
           p_mb, w_out, ln_g, ln_b):
    bsz, t, d = x.shape
    m = bsz * t
    rw_w = d
    mb_w = 2 * d
    rw_heads = rw_w // RW_HEAD
    mb_heads = mb_w // MB_HEAD
    conv_ch = mb_w + 2 * MB_GROUPS * MB_STATE
    splits = [3 * rw_w, 4 * rw_w, 4 * rw_w + mb_w, 4 * rw_w + mb_w + conv_ch,
              4 * rw_w + mb_w + conv_ch + 2 * mb_heads]
    hs = (bsz, t, rw_heads, RW_HEAD)

    h = _layer_norm(x, ln_in_g, ln_in_b)
    v_first = None
    for l in range(DEPTH):
        hb = h.reshape(m, d).astype(BF16)
        wl = w_in[l].astype(BF16)
        bounds = [0] + splits + [wl.shape[1]]
        p_rkv, g_rw, z, xbc, dt_raw, gates = [
            _matmul(hb, wl[:, bounds[i]:bounds[i + 1]]).reshape(bsz, t, -1) for i in range(6)]

        p_rkv = p_rkv + mu_rkv[l].reshape(-1) * _centred_delta(p_rkv)
        r, k, v = jnp.split(p_rkv, 3, axis=-1)
        hd = _centred_delta(h)
        xw = h + hd * mu_wa[l, 0]
        xa = h + hd * mu_wa[l, 1]
        w_lora = jnp.stack(_lora(xw, [w1[l, 0], w1[l, 1]], [w2[l, 0], w2[l, 1]], act=jnp.tanh))
        w_log = -jax.nn.softplus(-(w0[l][:, None, None, :] + w_lora)) - 0.5
        lw = -jnp.exp(w_log)
        a_sig = jax.nn.sigmoid(a0[l][:, None, None, :]
                               + jnp.stack(_lora(xa, [a1[l, 0], a1[l, 1]], [a2[l, 0], a2[l, 1]])))
        if l == 0:
            v_first = v
        else:
            xv = h + hd * mu_v[l - 1]
            vg = jax.nn.sigmoid(v0[l - 1] + _lora(xv, [v1[l - 1]], [v2[l - 1]])[0])
            v = v + (v_first - v) * vg
        kk = (k * k_k[l]).reshape(hs)
        kk = kk / jnp.maximum(jnp.sqrt(jnp.sum(kk * kk, -1, keepdims=True)), 1e-12)
        kk = kk.reshape(bsz, t, rw_w)
        y_f, y_b = _rwkv_scan(r, k, v, kk, lw, a_sig, k_a[l])
        y = (y_f + y_b).reshape(hs)
        mu = jnp.mean(y, -1, keepdims=True)
        yc = y - mu
        var = jnp.mean(yc * yc, -1, keepdims=True)
        gn = (yc * lax.rsqrt(var + RW_GN_EPS)).reshape(bsz, t, rw_w) * lnx_g[l] + lnx_b[l]
        k_dir = k[None] * (1.0 + (a_sig - 1.0) * k_a[l])
        bonus = jnp.sum((r[None] * k_dir).reshape((2,) + hs) * r_k[l], axis=(0, -1))[..., None] * v.reshape(hs)
        o_rw = (gn + bonus.reshape(bsz, t, rw_w)) * jax.nn.silu(g_rw)

        xp = jnp.pad(xbc, ((0, 0), (CONV_W // 2, CONV_W // 2), (0, 0)))
        conv = conv_b[l] + sum(xp[:, j:j + t] * conv_w[l, j] for j in range(CONV_W))
        xbc_c = jax.nn.silu(conv)
        dt = jax.nn.softplus(dt_raw.reshape(bsz, t, 2, mb_heads) + dt_bias[l])
        a_neg = -jnp.exp(a_log[l])
        ys_f, ys_b = _ssd_scan(xbc_c, dt, a_neg)
        xm = xbc_c[..., :mb_w]
        skip = (d_skip[l][:, None] * xm.reshape(bsz, t, mb_heads, MB_HEAD)).reshape(bsz, t, mb_w)
        ym = (ys_f + ys_b + skip) * jax.nn.silu(z)
        yg = ym.reshape(bsz, t, MB_GROUPS, mb_w // MB_GROUPS)
        yg = yg * lax.rsqrt(jnp.mean(yg * yg, -1, keepdims=True) + RMS_EPS)
        o_mb = yg.reshape(bsz, t, mb_w) * mb_norm_g[l]

        g_a, g_b = jnp.split(gates, 2, axis=-1)
        pr = _matmul(o_rw.reshape(m, rw_w).astype(BF16), p_rw[l].astype(BF16)).reshape(bsz, t, d)
        pm = _matmul(o_mb.reshape(m, mb_w).astype(BF16), p_mb[l].astype(BF16), tm=512).reshape(bsz, t, d)
        mix = jax.nn.sigmoid(g_a) * pr + jax.nn.sigmoid(g_b) * pm
        out = _matmul(mix.reshape(m, d).astype(BF16), w_out[l].astype(BF16)).reshape(bsz, t, d)
        h = _layer_norm(ALPHA * h + out, ln_g[l], ln_b[l])
    return h


def kernel(x_prompt, x_sample, ln_in_g, ln_in_b, w_in, mu_rkv, mu_wa, w0, w1, w2, a0, a1, a2, mu_v, v0, v1, v2, k_k, k_a, r_k, lnx_g, lnx_b, p_rw, conv_w, conv_b, dt_bias, a_log, d_skip, mb_norm_g, p_mb, w_out, ln_g, ln_b):
    assert x_prompt.shape[1:] == x_sample.shape[1:]
    nb = x_prompt.shape[0]
    x = jnp.concatenate([x_prompt, x_sample], axis=0)
    y = _trunk(x, ln_in_g, ln_in_b, w_in, mu_rkv, mu_wa, w0, w1, w2, a0, a1, a2, mu_v, v0, v1, v2,
               k_k, k_a, r_k, lnx_g, lnx_b, p_rw, conv_w, conv_b, dt_bias, a_log, d_skip, mb_norm_g,
               p_mb, w_out, ln_g, ln_b)
    y = y.astype(x_prompt.dtype)
    return (y[:nb], y[nb:])
```

```python
import functools
import math

import numpy as np
import jax
import jax.numpy as jnp
from jax import lax
from jax.experimental import pallas as pl
from jax.experimental.pallas import tpu as pltpu

F32 = jnp.float32
BF16 = jnp.bfloat16

DEPTH = 2
RW_HEAD = 64
RW_GN_EPS = 64e-5
MB_HEAD = 64
MB_GROUPS = 8
MB_STATE = 128
CONV_W = 7
ALPHA = (2 * DEPTH) ** 0.25
LN_EPS = 1e-5
RMS_EPS = 1e-5

LANES = 128
RW_CHUNK = 64
SSD_CHUNK = 128
VMEM_LIMIT = 56 * 1024 * 1024


def _mm_kernel(x_ref, w_ref, o_ref):
    o_ref[...] = jnp.dot(x_ref[...], w_ref[...],
                         preferred_element_type=F32).astype(o_ref.dtype)


def _matmul(x, w, *, tm=1024, tn=512, out_dtype=F32):
    m, k = x.shape
    n = w.shape[1]
    tm = min(tm, m)
    tn = min(tn, n)
    assert m % tm == 0 and n % tn == 0
    return pl.pallas_call(
        _mm_kernel,
        grid=(m // tm, n // tn),
        in_specs=[pl.BlockSpec((tm, k), lambda i, j: (i, 0)),
                  pl.BlockSpec((k, tn), lambda i, j: (0, j))],
        out_specs=pl.BlockSpec((tm, tn), lambda i, j: (i, j)),
        out_shape=jax.ShapeDtypeStruct((m, n), out_dtype),
        compiler_params=pltpu.CompilerParams(
            dimension_semantics=("parallel", "parallel"),
            vmem_limit_bytes=VMEM_LIMIT),
        name="dense_matmul",
    )(x, w)


def _dotf(a, b, exact=False):
    if exact:
        return jnp.dot(a, b, preferred_element_type=F32,
                       precision=lax.Precision.HIGHEST)
    return jnp.dot(a.astype(BF16), b.astype(BF16), preferred_element_type=F32)


def _dot_nt(a, b):
    return lax.dot_general(a.astype(BF16), b.astype(BF16),
                           (((1,), (1,)), ((), ())), preferred_element_type=F32)


def _dot_tn(a, b):
    return lax.dot_general(a.astype(BF16), b.astype(BF16),
                           (((0,), (0,)), ((), ())), preferred_element_type=F32)


_M_SAME, _M_EYE, _M_STRICT, _M_INCL, _M_BASE, _M_LEVEL = 0, 1, 2, 3, 6, 7


def _rwkv_masks():
    c = RW_CHUNK
    i = np.arange(LANES)
    row, col = i[:, None], i[None, :]
    same = (row >= c) == (col >= c)
    tl, ts = row % c, col % c
    out = [same, row == col,
           same & (tl > ts), same & (tl >= ts),
           same & (tl < ts), same & (tl <= ts),
           same & ((tl // 2) == (ts // 2))]
    s = 2
    while s < c:
        out.append(same & ((tl // (2 * s)) == (ts // (2 * s))) & ((tl // s) != (ts // s)))
        s *= 2
    return np.stack(out).astype(np.float32)


def _rwkv_cum():
    i = np.arange(RW_CHUNK)
    return np.stack([i[None, :] <= i[:, None], i[None, :] >= i[:, None]]).astype(np.float32)


def _rwkv_local(ci, refs, row0, reverse, ka, msk_ref, cum_ref, scr):
    mw_s, n_s, q_s, yl_s, dec_s = scr
    r_ref, k_ref, v_ref, kk_ref, lw_ref, a_ref = refs
    c = RW_CHUNK
    d = 1 if reverse else 0
    sl = (0, pl.ds(row0, c), slice(None))
    r = r_ref[sl]
    k = k_ref[sl]
    v = v_ref[sl]
    kk = kk_ref[sl]
    lw = lw_ref[sl]
    a = a_ref[sl]
    same = msk_ref[_M_SAME]
    strict = msk_ref[_M_STRICT + 2 * d]
    incl = msk_ref[_M_INCL + 2 * d]

    cl = _dotf(cum_ref[d], lw, exact=True)
    yield
    ce = cl - lw
    ct = cl[0:1, :] if reverse else cl[c - 1:c, :]
    kd = k * (1.0 + (a - 1.0) * ka)
    bv = kk * a
    einv = jnp.exp(-cl)
    dec_end = jnp.exp(ct - cl)
    dec_s[ci] = jnp.exp(ct)

    def bd(x):
        return jnp.concatenate([x, x], axis=0) * same

    xa = bd(-kk * jnp.exp(ce)).astype(BF16)
    xr = bd(r * jnp.exp(cl))
    vb = bd(v).astype(BF16)
    bh = bd(bv * dec_end).astype(BF16)
    kh = bd(kd * dec_end).astype(BF16)

    gram = _dot_nt(jnp.concatenate([xa, xr.astype(BF16)], axis=0),
                   jnp.concatenate([bd(bv * einv), bd(kd * einv)], axis=0))
    yield
    a_ab = gram[:LANES, :LANES] * strict
    a_ak = gram[:LANES, LANES:] * strict
    a_rb = (gram[LANES:, :LANES] * incl).astype(BF16)
    a_rk = (gram[LANES:, LANES:] * incl).astype(BF16)

    inv = msk_ref[_M_EYE] + a_ab * msk_ref[_M_BASE]
    g = _dotf(a_ak, vb)
    yk = _dotf(a_rk, vb)
    for lvl in range(int(math.log2(c)) - 1):
        off = (a_ab * msk_ref[_M_LEVEL + lvl]).astype(BF16)
        ib = inv.astype(BF16)
        t = _dotf(ib, off)
        yield
        t = _dotf(t, ib)
        yield
        inv = inv + t

    wu = _dotf(inv, jnp.concatenate([xa, g.astype(BF16)], axis=1))
    yield
    wu = wu.astype(BF16)
    mw = _dot_tn(wu[:, :LANES], bh)
    nn = _dot_tn(jnp.concatenate([wu[:, LANES:], vb], axis=0),
                 jnp.concatenate([bh, kh], axis=0))
    qy = _dotf(a_rb, wu)
    yield
    mw_s[ci] = mw.astype(BF16)
    n_s[ci] = nn
    q_s[ci] = (xr + qy[:, :LANES]).astype(BF16)
    yl = qy[:, LANES:] + yk
    yl_s[ci] = yl[:c] + yl[c:]


def _rwkv_state_step(ci, y_ref, row0, h_ref, scr):
    mw_s, n_s, q_s, yl_s, dec_s = scr
    c = RW_CHUNK
    h = h_ref[...]
    hb = h.astype(BF16)
    y = _dot_nt(q_s[ci], hb)
    y_ref[0, pl.ds(row0, c), :] = y[:c] + y[c:] + yl_s[ci]
    h_ref[...] = h * dec_s[ci] + _dotf(hb, mw_s[ci]) + n_s[ci]


def _rwkv_kernel(ka_ref, msk_ref, cum_ref, *refs, n_chunks):
    fwd, bwd = refs[0:6], refs[6:12]
    yf_ref, yb_ref, hf_ref, hb_ref = refs[12:16]
    scr = refs[16:21]

    @pl.when(pl.program_id(2) == 0)
    def _():
        hf_ref[...] = jnp.zeros_like(hf_ref)
        hb_ref[...] = jnp.zeros_like(hb_ref)

    ka = ka_ref[...]
    stages = []
    for j in range(n_chunks):
        stages.append(_rwkv_local(j, fwd, j * RW_CHUNK, False, ka, msk_ref, cum_ref, scr))
        stages.append(_rwkv_local(n_chunks + j, bwd, j * RW_CHUNK, True, ka, msk_ref, cum_ref, scr))
    while stages:
        live = []
        for gen in stages:
            if next(gen, StopIteration) is not StopIteration:
                live.append(gen)
        stages = live
    for j in range(n_chunks):
        _rwkv_state_step(j, yf_ref, j * RW_CHUNK, hf_ref, scr)
        jb = n_chunks - 1 - j
        _rwkv_state_step(n_chunks + jb, yb_ref, jb * RW_CHUNK, hb_ref, scr)


def _rwkv_scan(r, k, v, kk, lw, a, k_a, *, tb=512):
    bsz, t, w = r.shape
    tb = min(tb, t)
    nb = t // tb
    nch = tb // RW_CHUNK
    assert t % tb == 0 and tb % RW_CHUNK == 0 and w % LANES == 0
    blk = (1, tb, LANES)
    f_map = lambda p, b, i: (b, i, p)
    b_map = lambda p, b, i: (b, nb - 1 - i, p)
    spec_f = pl.BlockSpec(blk, f_map)
    spec_b = pl.BlockSpec(blk, b_map)
    msk = jnp.asarray(_rwkv_masks())
    cum = jnp.asarray(_rwkv_cum())
    out = jax.ShapeDtypeStruct((bsz, t, w), F32)
    return pl.pallas_call(
        functools.partial(_rwkv_kernel, n_chunks=nch),
        grid=(w // LANES, bsz, nb),
        in_specs=[pl.BlockSpec((1, LANES), lambda p, b, i: (0, p)),
                  pl.BlockSpec(msk.shape, lambda p, b, i: (0, 0, 0)),
                  pl.BlockSpec(cum.shape, lambda p, b, i: (0, 0, 0))]
                 + [spec_f] * 6 + [spec_b] * 6,
        out_specs=[spec_f, spec_b],
        out_shape=[out, out],
        scratch_shapes=[pltpu.VMEM((LANES, LANES), F32), pltpu.VMEM((LANES, LANES), F32),
                        pltpu.VMEM((2 * nch, LANES, LANES), BF16),
                        pltpu.VMEM((2 * nch, LANES, LANES), F32),
                        pltpu.VMEM((2 * nch, LANES, LANES), BF16),
                        pltpu.VMEM((2 * nch, RW_CHUNK, LANES), F32),
                        pltpu.VMEM((2 * nch, 1, LANES), F32)],
        compiler_params=pltpu.CompilerParams(
            dimension_semantics=("parallel", "parallel", "arbitrary"),
            vmem_limit_bytes=VMEM_LIMIT),
        name="rwkv7_scan",
    )(k_a.reshape(1, w), msk, cum, r, k, v, kk, lw[0], a[0], r, k, v, kk, lw[1], a[1])


def _ssd_chunk(refs, y_ref, s_ref, row0, reverse):
    dtm_ref, adtm_ref, adth_ref, x_ref, b_ref, c_ref = refs
    c = SSD_CHUNK
    rows = pl.ds(row0, c)
    dt_tm = dtm_ref[0, 0, 0, rows, :]
    adt_tm = adtm_ref[0, 0, 0, rows, :]
    adt_hm = adth_ref[0, 0, 0, :, rows]
    bc = b_ref[0, rows, :]
    cc = c_ref[0, rows, :]

    li = lax.broadcasted_iota(jnp.int32, (c, c), 0)
    si = lax.broadcasted_iota(jnp.int32, (c, c), 1)
    mask = (si >= li) if reverse else (si <= li)
    cum_m = mask.astype(F32)
    cum_t = ((li >= si) if reverse else (li <= si)).astype(F32)
    acum_tm = _dotf(cum_m, adt_tm, exact=True)
    acum_hm = _dotf(adt_hm, cum_t, exact=True)
    tot = acum_hm[:, 0:1] if reverse else acum_hm[:, c - 1:c]

    cb = _dot_nt(cc, bc)
    bt = bc.T
    lane = lax.broadcasted_iota(jnp.int32, (c, LANES), 1)
    first = lane < MB_HEAD
    lane1 = lax.broadcasted_iota(jnp.int32, (1, LANES), 1) < MB_HEAD

    n_pairs = x_ref.shape[2] // LANES
    for q in range(n_pairs):
        e1, e2 = 2 * q, 2 * q + 1
        col1 = jnp.broadcast_to(acum_tm[:, e1:e1 + 1], (c, LANES))
        col2 = jnp.broadcast_to(acum_tm[:, e2:e2 + 1], (c, LANES))
        colp = jnp.where(first, col1, col2)
        dtp = jnp.where(first, jnp.broadcast_to(dt_tm[:, e1:e1 + 1], (c, LANES)),
                        jnp.broadcast_to(dt_tm[:, e2:e2 + 1], (c, LANES)))
        totp = jnp.where(lane1, jnp.broadcast_to(tot[e1:e1 + 1, :], (1, LANES)),
                         jnp.broadcast_to(tot[e2:e2 + 1, :], (1, LANES)))
        xp = x_ref[0, rows, q * LANES:(q + 1) * LANES] * dtp
        l1 = jnp.where(mask, jnp.exp(jnp.minimum(col1 - acum_hm[e1:e1 + 1, :], 0.0)), 0.0)
        l2 = jnp.where(mask, jnp.exp(jnp.minimum(col2 - acum_hm[e2:e2 + 1, :], 0.0)), 0.0)
        yd = jnp.where(first, _dotf(l1 * cb, xp), _dotf(l2 * cb, xp))
        st = s_ref[q]
        yo = jnp.exp(colp) * _dotf(cc, st)
        y_ref[0, rows, q * LANES:(q + 1) * LANES] = yd + yo
        s_ref[q] = jnp.exp(totp) * st + _dotf(bt, jnp.exp(totp - colp) * xp)


def _ssd_kernel(*refs, n_chunks):
    fwd, bwd = refs[0:6], refs[6:12]
    yf_ref, yb_ref, sf_ref, sb_ref = refs[12:16]

    @pl.when(pl.program_id(2) == 0)
    def _():
        sf_ref[...] = jnp.zeros_like(sf_ref)
        sb_ref[...] = jnp.zeros_like(sb_ref)

    def body(j, carry):
        _ssd_chunk(fwd, yf_ref, sf_ref, pl.multiple_of(j * SSD_CHUNK, SSD_CHUNK), False)
        jb = n_chunks - 1 - j
        _ssd_chunk(bwd, yb_ref, sb_ref, pl.multiple_of(jb * SSD_CHUNK, SSD_CHUNK), True)
        return carry

    lax.fori_loop(0, n_chunks, body, 0)


def _ssd_scan(xbc, dt, a_neg, *, tb=512):
    bsz, t, _ = xbc.shape
    g = MB_GROUPS
    nh = dt.shape[-1]
    e = nh // g
    w = nh * MB_HEAD
    gw = e * MB_HEAD
    tb = min(tb, t)
    nb = t // tb
    assert t % tb == 0 and tb % SSD_CHUNK == 0 and MB_STATE == LANES
    dt_g = dt.reshape(bsz, t, 2, g, e)
    adt_g = dt_g * a_neg.reshape(2, g, e)
    dt_tm = jnp.transpose(dt_g, (2, 0, 3, 1, 4))
    adt_tm = jnp.transpose(adt_g, (2, 0, 3, 1, 4))
    adt_hm = jnp.transpose(adt_g, (2, 0, 3, 4, 1))

    def specs(d, tmap):
        return [
            pl.BlockSpec((1, 1, 1, tb, e), lambda b, gi, i: (d, b, gi, tmap(i), 0)),
            pl.BlockSpec((1, 1, 1, tb, e), lambda b, gi, i: (d, b, gi, tmap(i), 0)),
            pl.BlockSpec((1, 1, 1, e, tb), lambda b, gi, i: (d, b, gi, 0, tmap(i))),
            pl.BlockSpec((1, tb, gw), lambda b, gi, i: (b, tmap(i), gi)),
            pl.BlockSpec((1, tb, MB_STATE), lambda b, gi, i: (b, tmap(i), w // MB_STATE + gi)),
            pl.BlockSpec((1, tb, MB_STATE), lambda b, gi, i: (b, tmap(i), w // MB_STATE + g + gi)),
        ]

    fmap = lambda i: i
    bmap = lambda i: nb - 1 - i
    out = jax.ShapeDtypeStruct((bsz, t, w), F32)
    n_pairs = gw // LANES
    return pl.pallas_call(
        functools.partial(_ssd_kernel, n_chunks=tb // SSD_CHUNK),
        grid=(bsz, g, nb),
        in_specs=specs(0, fmap) + specs(1, bmap),
        out_specs=[pl.BlockSpec((1, tb, gw), lambda b, gi, i: (b, i, gi)),
                   pl.BlockSpec((1, tb, gw), lambda b, gi, i: (b, nb - 1 - i, gi))],
        out_shape=[out, out],
        scratch_shapes=[pltpu.VMEM((n_pairs, MB_STATE, LANES), F32),
                        pltpu.VMEM((n_pairs, MB_STATE, LANES), F32)],
        compiler_params=pltpu.CompilerParams(
            dimension_semantics=("parallel", "parallel", "arbitrary"),
            vmem_limit_bytes=VMEM_LIMIT),
        name="ssd_scan",
    )(dt_tm, adt_tm, adt_hm, xbc, xbc, xbc, dt_tm, adt_tm, adt_hm, xbc, xbc, xbc)


def _layer_norm(x, g, b):
    mu = jnp.mean(x, -1, keepdims=True)
    xc = x - mu
    var = jnp.mean(xc * xc, -1, keepdims=True)
    return xc * lax.rsqrt(var + LN_EPS) * g + b


def _pad_cols(wm, n):
    return jnp.pad(wm, ((0, 0), (0, n - wm.shape[1])))


def _pad_rows(wm, n):
    return jnp.pad(wm, ((0, n - wm.shape[0]), (0, 0)))


def _stack_rows(vecs, rows=8):
    a = jnp.stack([v.reshape(-1).astype(F32) for v in vecs])
    return _pad_rows(a, -(-a.shape[0] // rows) * rows)


HALO = 8


def _halo_specs(tm, tn, n_rows, col_off=0):
    last8 = n_rows // HALO - 1
    r8 = tm // HALO
    return [pl.BlockSpec((tm, tn), lambda i, j: (i, col_off + j)),
            pl.BlockSpec((HALO, tn), lambda i, j: (jnp.maximum(i * r8 - 1, 0), col_off + j)),
            pl.BlockSpec((HALO, tn), lambda i, j: (jnp.minimum((i + 1) * r8, last8), col_off + j))]


def _stage_rows(ext_ref, x_ref, prev_ref, next_ref, tiles_per_seq):
    tm = x_ref.shape[0]
    i = pl.program_id(0)
    first = (i % tiles_per_seq) == 0
    last = (i % tiles_per_seq) == tiles_per_seq - 1
    ext_ref[0:HALO, :] = jnp.where(first, 0.0, prev_ref[...])
    ext_ref[HALO:HALO + tm, :] = x_ref[...]
    ext_ref[HALO + tm:2 * HALO + tm, :] = jnp.where(last, 0.0, next_ref[...])


def _shifted(ext_ref, s, tm):
    return ext_ref[pl.ds(HALO + s, tm), :]


def _delta(ext_ref, tm):
    return 0.5 * (_shifted(ext_ref, -1, tm) + _shifted(ext_ref, 1, tm)) - _shifted(ext_ref, 0, tm)


def _sigmoid(x):
    return 1.0 / (1.0 + jnp.exp(-x))


def _softplus(x):
    return jnp.maximum(x, 0.0) + jnp.log(1.0 + jnp.exp(-jnp.abs(x)))


def _group_sum(x, e_ref):
    e = e_ref[...]
    outs = []
    for s in range(x.shape[1] // LANES):
        xs = x[:, s * LANES:(s + 1) * LANES]
        hi = xs.astype(BF16)
        lo = (xs - hi.astype(F32)).astype(BF16)
        outs.append(jnp.dot(hi, e, preferred_element_type=F32)
                    + jnp.dot(lo, e, preferred_element_type=F32))
    return outs[0] if len(outs) == 1 else jnp.concatenate(outs, axis=1)


def _head_indicator():
    i = np.arange(LANES) // RW_HEAD
    return jnp.asarray((i[:, None] == i[None, :]).astype(np.float32), dtype=BF16)


def _row_params(tm, seq_len):
    tm = min(tm, seq_len)
    assert seq_len % tm == 0 and tm % HALO == 0
    return tm, seq_len // tm


def _lora_in_kernel(h_ref, hp_ref, hn_ref, mu_ref, w_ref, o_ref, ext_ref, *, tiles_per_seq, n_lora):
    tm = h_ref.shape[0]
    _stage_rows(ext_ref, h_ref, hp_ref, hn_ref, tiles_per_seq)
    h = h_ref[...]
    hd = _delta(ext_ref, tm)
    src = (0, 0, 1, 1, 2)
    for i in range(n_lora):
        x = (h + hd * mu_ref[src[i]:src[i] + 1, :]).astype(BF16)
        mid = jnp.dot(x, w_ref[:, i * LANES:(i + 1) * LANES], preferred_element_type=F32)
        if i < 2:
            mid = jnp.tanh(mid)
        o_ref[:, i * LANES:(i + 1) * LANES] = mid.astype(o_ref.dtype)


def _lora_in(h, mus, w_first, seq_len, *, tm=512):
    m, d = h.shape
    tm, tps = _row_params(tm, seq_len)
    n_lora = len(w_first)
    w = jnp.concatenate([_pad_cols(x, LANES) for x in w_first], axis=1).astype(BF16)
    mu = _stack_rows(mus)
    return pl.pallas_call(
        functools.partial(_lora_in_kernel, tiles_per_seq=tps, n_lora=n_lora),
        grid=(m // tm, 1),
        in_specs=_halo_specs(tm, d, m) + [pl.BlockSpec(mu.shape, lambda i, j: (0, 0)),
                                          pl.BlockSpec(w.shape, lambda i, j: (0, 0))],
        out_specs=pl.BlockSpec((tm, n_lora * LANES), lambda i, j: (i, 0)),
        out_shape=jax.ShapeDtypeStruct((m, n_lora * LANES), BF16),
        scratch_shapes=[pltpu.VMEM((tm + 2 * HALO, d), F32)],
        compiler_params=pltpu.CompilerParams(
            dimension_semantics=("parallel", "arbitrary"), vmem_limit_bytes=VMEM_LIMIT),
        name="lora_in",
    )(h, h, h, mu, w)


_P_MU_R, _P_MU_K, _P_MU_V, _P_W0, _P_A0, _P_KK, _P_V0 = 0, 1, 2, 3, 5, 7, 8


def _rwkv_prep_kernel(*refs, tiles_per_seq, has_vres):
    (pr, prp, prn, pk, pkp, pkn, pv, pvp, pvn, mid_ref, w2_ref, par_ref, e_ref) = refs[:13]
    pos = 13
    vf_ref = None
    if has_vres:
        vf_ref = refs[pos]
        pos += 1
    r_o, k_o, v_o, kk_o, lw0_o, lw1_o, a0_o, a1_o = refs[pos:pos + 8]
    er, ek, ev = refs[pos + 8:pos + 11]
    tm = pr.shape[0]
    par = lambda i: par_ref[i:i + 1, :]

    outs = []
    for ext, x, xp, xn, mu in ((er, pr, prp, prn, _P_MU_R), (ek, pk, pkp, pkn, _P_MU_K),
                               (ev, pv, pvp, pvn, _P_MU_V)):
        _stage_rows(ext, x, xp, xn, tiles_per_seq)
        outs.append(x[...] + par(mu) * _delta(ext, tm))
    r, k, v = outs

    def lora(i):
        return jnp.dot(mid_ref[:, i * LANES:(i + 1) * LANES], w2_ref[i],
                       preferred_element_type=F32)

    for d, (lw_o, a_o) in enumerate(((lw0_o, a0_o), (lw1_o, a1_o))):
        w_log = -_softplus(-(par(_P_W0 + d) + lora(d))) - 0.5
        lw_o[...] = -jnp.exp(w_log)
        a_o[...] = _sigmoid(par(_P_A0 + d) + lora(2 + d))
    if has_vres:
        vg = _sigmoid(par(_P_V0) + lora(4))
        v = v + (vf_ref[...] - v) * vg
    kk = k * par(_P_KK)
    nrm = jnp.maximum(jnp.sqrt(_group_sum(kk * kk, e_ref)), 1e-12)
    r_o[...] = r
    k_o[...] = k
    v_o[...] = v
    kk_o[...] = kk / nrm


def _rwkv_prep(p_rkv, mid, w_second, params, v_first, seq_len, *, tm=512, tn=256):
    m, w3 = p_rkv.shape
    w = w3 // 3
    tm, tps = _row_params(tm, seq_len)
    tn = min(tn, w)
    nj = w // tn
    assert w % tn == 0 and tn % LANES == 0
    has_vres = v_first is not None
    n_lora = len(w_second)
    w2 = jnp.stack([_pad_rows(x, LANES) for x in w_second]).astype(BF16)
    par = _stack_rows(params)
    tile = pl.BlockSpec((tm, tn), lambda i, j: (i, j))
    in_specs = (_halo_specs(tm, tn, m, 0) + _halo_specs(tm, tn, m, nj) + _halo_specs(tm, tn, m, 2 * nj)
                + [pl.BlockSpec((tm, mid.shape[1]), lambda i, j: (i, 0)),
                   pl.BlockSpec((n_lora, LANES, tn), lambda i, j: (0, 0, j)),
                   pl.BlockSpec((par.shape[0], tn), lambda i, j: (0, j)),
                   pl.BlockSpec((LANES, LANES), lambda i, j: (0, 0))])
    args = [p_rkv] * 9 + [mid, w2, par, _head_indicator()]
    if has_vres:
        in_specs.append(tile)
        args.append(v_first)
    out = jax.ShapeDtypeStruct((m, w), F32)
    return pl.pallas_call(
        functools.partial(_rwkv_prep_kernel, tiles_per_seq=tps, has_vres=has_vres),
        grid=(m // tm, nj),
        in_specs=in_specs,
        out_specs=[tile] * 8,
        out_shape=[out] * 8,
        scratch_shapes=[pltpu.VMEM((tm + 2 * HALO, tn), F32)] * 3,
        compiler_params=pltpu.CompilerParams(
            dimension_semantics=("parallel", "parallel"), vmem_limit_bytes=VMEM_LIMIT),
        name="rwkv_prep",
    )(*args)


def _rwkv_post_kernel(yf, yb, r, k, v, a0, a1, g, par_ref, e_ref, o_ref):
    par = lambda i: par_ref[i:i + 1, :]
    lnx_g, lnx_b, ka, rk = par(0), par(1), par(2), par(3)
    inv_n = 1.0 / RW_HEAD
    y = yf[...] + yb[...]
    yc = y - _group_sum(y, e_ref) * inv_n
    var = _group_sum(yc * yc, e_ref) * inv_n
    gn = yc * lax.rsqrt(var + RW_GN_EPS) * lnx_g + lnx_b
    kdir = (1.0 + (a0[...] - 1.0) * ka) + (1.0 + (a1[...] - 1.0) * ka)
    bonus = _group_sum(r[...] * k[...] * kdir * rk, e_ref) * v[...]
    gate = g[...]
    o_ref[...] = ((gn + bonus) * (gate * _sigmoid(gate))).astype(o_ref.dtype)


def _rwkv_post(y_f, y_b, r, k, v, a0, a1, g_rw, params, *, tm=512, tn=256):
    m, w = r.shape
    tm = min(tm, m)
    tn = min(tn, w)
    par = _stack_rows(params)
    tile = pl.BlockSpec((tm, tn), lambda i, j: (i, j))
    return pl.pallas_call(
        _rwkv_post_kernel,
        grid=(m // tm, w // tn),
        in_specs=[tile] * 8 + [pl.BlockSpec((par.shape[0], tn), lambda i, j: (0, j)),
                               pl.BlockSpec((LANES, LANES), lambda i, j: (0, 0))],
        out_specs=tile,
        out_shape=jax.ShapeDtypeStruct((m, w), BF16),
        compiler_params=pltpu.CompilerParams(
            dimension_semantics=("parallel", "parallel"), vmem_limit_bytes=VMEM_LIMIT),
        name="rwkv_post",
    )(y_f, y_b, r, k, v, a0, a1, g_rw, par, _head_indicator())


def _conv_silu_kernel(x_ref, xp_ref, xn_ref, w_ref, o_ref, ext_ref, *, tiles_per_seq):
    tm = x_ref.shape[0]
    _stage_rows(ext_ref, x_ref, xp_ref, xn_ref, tiles_per_seq)
    acc = w_ref[CONV_W:CONV_W + 1, :] + _shifted(ext_ref, -(CONV_W // 2), tm) * w_ref[0:1, :]
    for j in range(1, CONV_W):
        acc = acc + _shifted(ext_ref, j - CONV_W // 2, tm) * w_ref[j:j + 1, :]
    o_ref[...] = acc * _sigmoid(acc)


def _conv_silu(x, conv_w, conv_b, seq_len, *, tm=512, tn=512):
    m, ch = x.shape
    tm, tps = _row_params(tm, seq_len)
    tn = min(tn, ch)
    assert ch % tn == 0 and CONV_W // 2 <= HALO
    wb = jnp.concatenate([conv_w, conv_b[None, :]], axis=0)
    return pl.pallas_call(
        functools.partial(_conv_silu_kernel, tiles_per_seq=tps),
        grid=(m // tm, ch // tn),
        in_specs=_halo_specs(tm, tn, m) + [pl.BlockSpec((CONV_W + 1, tn), lambda i, j: (0, j))],
        out_specs=pl.BlockSpec((tm, tn), lambda i, j: (i, j)),
        out_shape=jax.ShapeDtypeStruct((m, ch), F32),
        scratch_shapes=[pltpu.VMEM((tm + 2 * HALO, tn), F32)],
        compiler_params=pltpu.CompilerParams(
            dimension_semantics=("parallel", "parallel"), vmem_limit_bytes=VMEM_LIMIT),
        name="conv_silu",
    )(x, x, x, wb)


def _mamba_post_kernel(yf, yb, xm, z, par_ref, o_ref):
    d_skip, norm_g = par_ref[0:1, :], par_ref[1:2, :]
    gate = z[...]
    y = (yf[...] + yb[...] + d_skip * xm[...]) * (gate * _sigmoid(gate))
    ms = jnp.mean(y * y, axis=-1, keepdims=True)
    o_ref[...] = (y * lax.rsqrt(ms + RMS_EPS) * norm_g).astype(o_ref.dtype)


def _mamba_post(ys_f, ys_b, xbc_c, z, d_skip_rep, norm_g, *, tm=512):
    m, w = z.shape
    tm = min(tm, m)
    gw = w // MB_GROUPS
    par = _stack_rows([d_skip_rep, norm_g])
    tile = pl.BlockSpec((tm, gw), lambda i, j: (i, j))
    return pl.pallas_call(
        _mamba_post_kernel,
        grid=(m // tm, MB_GROUPS),
        in_specs=[tile] * 4 + [pl.BlockSpec((par.shape[0], gw), lambda i, j: (0, j))],
        out_specs=tile,
        out_shape=jax.ShapeDtypeStruct((m, w), BF16),
        compiler_params=pltpu.CompilerParams(
            dimension_semantics=("parallel", "parallel"), vmem_limit_bytes=VMEM_LIMIT),
        name="mamba_post",
    )(ys_f, ys_b, xbc_c, z, par)


def _merge_kernel(x_ref, w_ref, pr_ref, ga_ref, gb_ref, o_ref):
    pm = jnp.dot(x_ref[...], w_ref[...], preferred_element_type=F32)
    o_ref[...] = (_sigmoid(ga_ref[...]) * pr_ref[...] + _sigmoid(gb_ref[...]) * pm).astype(o_ref.dtype)


def _merge_matmul(o_mb, p_mb, pr, gates, *, tm=512, tn=512):
    m, k = o_mb.shape
    n = p_mb.shape[1]
    tm = min(tm, m)
    tn = min(tn, n)
    nj = n // tn
    tile = pl.BlockSpec((tm, tn), lambda i, j: (i, j))
    return pl.pallas_call(
        _merge_kernel,
        grid=(m // tm, nj),
        in_specs=[pl.BlockSpec((tm, k), lambda i, j: (i, 0)),
                  pl.BlockSpec((k, tn), lambda i, j: (0, j)),
                  tile, tile, pl.BlockSpec((tm, tn), lambda i, j: (i, nj + j))],
        out_specs=tile,
        out_shape=jax.ShapeDtypeStruct((m, n), BF16),
        compiler_params=pltpu.CompilerParams(
            dimension_semantics=("parallel", "parallel"), vmem_limit_bytes=VMEM_LIMIT),
        name="merge_matmul",
    )(o_mb, p_mb, pr, gates, gates)


def _out_ln_kernel(x_ref, w_ref, h_ref, par_ref, o_ref, ob_ref):
    out = jnp.dot(x_ref[...], w_ref[...], preferred_element_type=F32)
    res = ALPHA * h_ref[...] + out
    mu = jnp.mean(res, axis=-1, keepdims=True)
    xc = res - mu
    var = jnp.mean(xc * xc, axis=-1, keepdims=True)
    hn = xc * lax.rsqrt(var + LN_EPS) * par_ref[0:1, :] + par_ref[1:2, :]
    o_ref[...] = hn
    ob_ref[...] = hn.astype(ob_ref.dtype)


def _out_ln(mix, w_out, h, ln_g, ln_b, *, tm=256):
    m, k = mix.shape
    n = w_out.shape[1]
    tm = min(tm, m)
    par = _stack_rows([ln_g, ln_b])
    return pl.pallas_call(
        _out_ln_kernel,
        grid=(m // tm,),
        in_specs=[pl.BlockSpec((tm, k), lambda i: (i, 0)),
                  pl.BlockSpec((k, n), lambda i: (0, 0)),
                  pl.BlockSpec((tm, n), lambda i: (i, 0)),
                  pl.BlockSpec(par.shape, lambda i: (0, 0))],
        out_specs=[pl.BlockSpec((tm, n), lambda i: (i, 0))] * 2,
        out_shape=[jax.ShapeDtypeStruct((m, n), F32), jax.ShapeDtypeStruct((m, n), BF16)],
        compiler_params=pltpu.CompilerParams(
            dimension_semantics=("parallel",), vmem_limit_bytes=VMEM_LIMIT),
        name="out_ln",
    )(mix, w_out, h, par)


def _trunk(x, ln_in_g, ln_in_b, w_in, mu_rkv, mu_wa, w0, w1, w2, a0, a1, a2, mu_v, v0, v1, v2,
           k_k, k_a, r_k, lnx_g, lnx_b, p_rw, conv_w, conv_b, dt_bias, a_log, d_skip, mb_norm_g,
           p_mb, w_out, ln_g, ln_b):
    bsz, t, d = x.shape
    m = bsz * t
    rw_w = d
    mb_w = 2 * d
    rw_heads = rw_w // RW_HEAD
    mb_heads = mb_w // MB_HEAD
    conv_ch = mb_w + 2 * MB_GROUPS * MB_STATE
    splits = [3 * rw_w, 4 * rw_w, 4 * rw_w + mb_w, 4 * rw_w + mb_w + conv_ch,
              4 * rw_w + mb_w + conv_ch + 2 * mb_heads]
    seq = lambda a: a.reshape(bsz, t, a.shape[-1])

    h = _layer_norm(x, ln_in_g, ln_in_b).reshape(m, d)
    hb = h.astype(BF16)
    v_first = None
    for l in range(DEPTH):
        wl = w_in[l].astype(BF16)
        bounds = [0] + splits + [wl.shape[1]]
        p_rkv, g_rw, z, xbc, dt_raw, gates = [
            _matmul(hb, wl[:, bounds[i]:bounds[i + 1]]) for i in range(6)]

        mus = [mu_wa[l, 0], mu_wa[l, 1]]
        w_first = [w1[l, 0], w1[l, 1], a1[l, 0], a1[l, 1]]
        w_second = [w2[l, 0], w2[l, 1], a2[l, 0], a2[l, 1]]
        params = [mu_rkv[l, 0], mu_rkv[l, 1], mu_rkv[l, 2], w0[l, 0], w0[l, 1], a0[l, 0], a0[l, 1], k_k[l]]
        if l > 0:
            mus.append(mu_v[l - 1])
            w_first.append(v1[l - 1])
            w_second.append(v2[l - 1])
            params.append(v0[l - 1])
        mid = _lora_in(h, mus, w_first, t)
        r, k, v, kk, lw0, lw1, a_0, a_1 = _rwkv_prep(p_rkv, mid, w_second, params, v_first, t)
        if l == 0:
            v_first = v
        y_f, y_b = _rwkv_scan(seq(r), seq(k), seq(v), seq(kk), (seq(lw0), seq(lw1)),
                              (seq(a_0), seq(a_1)), k_a[l])
        o_rw = _rwkv_post(y_f.reshape(m, rw_w), y_b.reshape(m, rw_w), r, k, v, a_0, a_1, g_rw,
                          [lnx_g[l], lnx_b[l], k_a[l], r_k[l]])

        xbc_c = _conv_silu(xbc, conv_w[l], conv_b[l], t)
        dt = jax.nn.softplus(dt_raw.reshape(bsz, t, 2, mb_heads) + dt_bias[l])
        a_neg = -jnp.exp(a_log[l])
        ys_f, ys_b = _ssd_scan(seq(xbc_c), dt, a_neg)
        o_mb = _mamba_post(ys_f.reshape(m, mb_w), ys_b.reshape(m, mb_w), xbc_c, z,
                           jnp.repeat(d_skip[l], MB_HEAD), mb_norm_g[l])

        pr = _matmul(o_rw, p_rw[l].astype(BF16))
        mix = _merge_matmul(o_mb, p_mb[l].astype(BF16), pr, gates)
        h, hb = _out_ln(mix, w_out[l].astype(BF16), h, ln_g[l], ln_b[l])
    return h.reshape(bsz, t, d)


def kernel(x_prompt, x_sample, ln_in_g, ln_in_b, w_in, mu_rkv, mu_wa, w0, w1, w2, a0, a1, a2, mu_v, v0, v1, v2, k_k, k_a, r_k, lnx_g, lnx_b, p_rw, conv_w, conv_b, dt_bias, a_log, d_skip, mb_norm_g, p_mb, w_out, ln_g, ln_b):
    assert x_prompt.shape[1:] == x_sample.shape[1:]
    nb = x_prompt.shape[0]
    x = jnp.concatenate([x_prompt, x_sample], axis=0)
    y = _trunk(x, ln_in_g, ln_in_b, w_in, mu_rkv, mu_wa, w0, w1, w2, a0, a1, a2, mu_v, v0, v1, v2,
               k_k, k_a, r_k, lnx_g, lnx_b, p_rw, conv_w, conv_b, dt_bias, a_log, d_skip, mb_norm_g,
               p_mb, w_out, ln_g, ln_b)
    y = y.astype(x_prompt.dtype)
    return (y[:nb], y[nb:])
```

```python
import functools
import math

import numpy as np
import jax
import jax.numpy as jnp
from jax import lax
from jax.experimental import pallas as pl
from jax.experimental.pallas import tpu as pltpu

F32 = jnp.float32
BF16 = jnp.bfloat16

DEPTH = 2
RW_HEAD = 64
RW_GN_EPS = 64e-5
MB_HEAD = 64
MB_GROUPS = 8
MB_STATE = 128
CONV_W = 7
ALPHA = (2 * DEPTH) ** 0.25
LN_EPS = 1e-5
RMS_EPS = 1e-5

LANES = 128
RW_CHUNK = 64
SSD_CHUNK = 128
VMEM_LIMIT = 56 * 1024 * 1024


def _sigmoid(x):
    return 1.0 / (1.0 + jnp.exp(-x))


def _mm_kernel(x_ref, w_ref, o_ref, *, act):
    acc = jnp.dot(x_ref[...], w_ref[...], preferred_element_type=F32)
    if act == "sigmoid":
        acc = _sigmoid(acc)
    elif act == "silu":
        acc = acc * _sigmoid(acc)
    o_ref[...] = acc.astype(o_ref.dtype)


def _matmul(x, w, *, tm=1024, tn=512, out_dtype=F32, act=None):
    m, k = x.shape
    n = w.shape[1]
    tm = min(tm, m)
    tn = min(tn, n)
    assert m % tm == 0 and n % tn == 0
    return pl.pallas_call(
        functools.partial(_mm_kernel, act=act),
        grid=(m // tm, n // tn),
        in_specs=[pl.BlockSpec((tm, k), lambda i, j: (i, 0)),
                  pl.BlockSpec((k, tn), lambda i, j: (0, j))],
        out_specs=pl.BlockSpec((tm, tn), lambda i, j: (i, j)),
        out_shape=jax.ShapeDtypeStruct((m, n), out_dtype),
        compiler_params=pltpu.CompilerParams(
            dimension_semantics=("parallel", "parallel"),
            vmem_limit_bytes=VMEM_LIMIT),
        name="dense_matmul",
    )(x, w)


def _dotf(a, b, exact=False):
    if exact:
        return jnp.dot(a, b, preferred_element_type=F32,
                       precision=lax.Precision.HIGHEST)
    return jnp.dot(a.astype(BF16), b.astype(BF16), preferred_element_type=F32)


def _dot_nt(a, b):
    return lax.dot_general(a.astype(BF16), b.astype(BF16),
                           (((1,), (1,)), ((), ())), preferred_element_type=F32)


def _dot_tn(a, b):
    return lax.dot_general(a.astype(BF16), b.astype(BF16),
                           (((0,), (0,)), ((), ())), preferred_element_type=F32)


def _dot_split3(ones, x):
    ob = ones.astype(BF16)
    acc = None
    for _ in range(3):
        part = x.astype(BF16)
        x = x - part.astype(F32)
        term = jnp.dot(ob, part, preferred_element_type=F32)
        acc = term if acc is None else acc + term
    return acc


def _lockstep(stages):
    while stages:
        stages = [g for g in stages if next(g, StopIteration) is not StopIteration]


_M_SAME, _M_EYE, _M_STRICT, _M_INCL, _M_BASE, _M_LEVEL = 0, 1, 2, 3, 6, 7


def _rwkv_masks():
    c = RW_CHUNK
    i = np.arange(LANES)
    row, col = i[:, None], i[None, :]
    same = (row >= c) == (col >= c)
    tl, ts = row % c, col % c
    out = [same, row == col,
           same & (tl > ts), same & (tl >= ts),
           same & (tl < ts), same & (tl <= ts),
           same & ((tl // 2) == (ts // 2))]
    s = 2
    while s < c:
        out.append(same & ((tl // (2 * s)) == (ts // (2 * s))) & ((tl // s) != (ts // s)))
        s *= 2
    return np.stack(out).astype(np.float32)


def _rwkv_cum():
    i = np.arange(RW_CHUNK)
    return np.stack([i[None, :] <= i[:, None], i[None, :] >= i[:, None]]).astype(np.float32)


def _rwkv_local(ci, refs, row0, reverse, ka, msk_ref, cum_ref, scr):
    mw_s, n_s, q_s, yl_s, dec_s = scr
    r_ref, k_ref, v_ref, kk_ref, lw_ref, a_ref = refs
    c = RW_CHUNK
    d = 1 if reverse else 0
    sl = (0, pl.ds(row0, c), slice(None))
    r = r_ref[sl].astype(F32)
    k = k_ref[sl].astype(F32)
    v = v_ref[sl].astype(F32)
    kk = kk_ref[sl].astype(F32)
    lw = lw_ref[sl]
    a = a_ref[sl].astype(F32)
    same = msk_ref[_M_SAME]
    strict = msk_ref[_M_STRICT + 2 * d]
    incl = msk_ref[_M_INCL + 2 * d]

    cl = _dot_split3(cum_ref[d], lw)
    yield
    ce = cl - lw
    ct = cl[0:1, :] if reverse else cl[c - 1:c, :]
    kd = k * (1.0 + (a - 1.0) * ka)
    bv = kk * a
    einv = jnp.exp(-cl)
    dec_end = jnp.exp(ct - cl)
    dec_s[ci] = jnp.exp(ct)

    def bd(x):
        return jnp.concatenate([x, x], axis=0) * same

    xa = bd(-kk * jnp.exp(ce)).astype(BF16)
    xr = bd(r * jnp.exp(cl))
    vb = bd(v).astype(BF16)
    bh = bd(bv * dec_end).astype(BF16)
    kh = bd(kd * dec_end).astype(BF16)

    gram = _dot_nt(jnp.concatenate([xa, xr.astype(BF16)], axis=0),
                   jnp.concatenate([bd(bv * einv), bd(kd * einv)], axis=0))
    yield
    a_ab = gram[:LANES, :LANES] * strict
    a_ak = gram[:LANES, LANES:] * strict
    a_rb = (gram[LANES:, :LANES] * incl).astype(BF16)
    a_rk = (gram[LANES:, LANES:] * incl).astype(BF16)

    inv = msk_ref[_M_EYE] + a_ab * msk_ref[_M_BASE]
    g = _dotf(a_ak, vb)
    yk = _dotf(a_rk, vb)
    for lvl in range(int(math.log2(c)) - 1):
        off = (a_ab * msk_ref[_M_LEVEL + lvl]).astype(BF16)
        ib = inv.astype(BF16)
        t = _dotf(ib, off)
        yield
        t = _dotf(t, ib)
        yield
        inv = inv + t

    wu = _dotf(inv, jnp.concatenate([xa, g.astype(BF16)], axis=1))
    yield
    wu = wu.astype(BF16)
    mw = _dot_tn(wu[:, :LANES], bh)
    nn = _dot_tn(jnp.concatenate([wu[:, LANES:], vb], axis=0),
                 jnp.concatenate([bh, kh], axis=0))
    qy = _dotf(a_rb, wu)
    yield
    mw_s[ci] = mw.astype(BF16)
    n_s[ci] = nn
    q_s[ci] = (xr + qy[:, :LANES]).astype(BF16)
    yl = qy[:, LANES:] + yk
    yl_s[ci] = yl[:c] + yl[c:]


def _rwkv_state_step(ci, y_ref, row0, h_ref, scr):
    mw_s, n_s, q_s, yl_s, dec_s = scr
    c = RW_CHUNK
    h = h_ref[...]
    hb = h.astype(BF16)
    y = _dot_nt(q_s[ci], hb)
    y_ref[0, pl.ds(row0, c), :] = y[:c] + y[c:] + yl_s[ci]
    h_ref[...] = h * dec_s[ci] + _dotf(hb, mw_s[ci]) + n_s[ci]


def _rwkv_kernel(ka_ref, msk_ref, cum_ref, *refs, n_chunks):
    fwd, bwd = refs[0:6], refs[6:12]
    yf_ref, yb_ref, hf_ref, hb_ref = refs[12:16]
    scr = refs[16:21]

    @pl.when(pl.program_id(2) == 0)
    def _():
        hf_ref[...] = jnp.zeros_like(hf_ref)
        hb_ref[...] = jnp.zeros_like(hb_ref)

    ka = ka_ref[...]
    stages = []
    for j in range(n_chunks):
        stages.append(_rwkv_local(j, fwd, j * RW_CHUNK, False, ka, msk_ref, cum_ref, scr))
        stages.append(_rwkv_local(n_chunks + j, bwd, j * RW_CHUNK, True, ka, msk_ref, cum_ref, scr))
    _lockstep(stages)
    for j in range(n_chunks):
        _rwkv_state_step(j, yf_ref, j * RW_CHUNK, hf_ref, scr)
        jb = n_chunks - 1 - j
        _rwkv_state_step(n_chunks + jb, yb_ref, jb * RW_CHUNK, hb_ref, scr)


def _rwkv_scan(r, k, v, kk, lw, a, k_a, *, tb=512):
    bsz, t, w = r.shape
    tb = min(tb, t)
    nb = t // tb
    nch = tb // RW_CHUNK
    assert t % tb == 0 and tb % RW_CHUNK == 0 and w % LANES == 0
    blk = (1, tb, LANES)
    f_map = lambda p, b, i: (b, i, p)
    b_map = lambda p, b, i: (b, nb - 1 - i, p)
    spec_f = pl.BlockSpec(blk, f_map)
    spec_b = pl.BlockSpec(blk, b_map)
    msk = jnp.asarray(_rwkv_masks())
    cum = jnp.asarray(_rwkv_cum())
    out = jax.ShapeDtypeStruct((bsz, t, w), F32)
    return pl.pallas_call(
        functools.partial(_rwkv_kernel, n_chunks=nch),
        grid=(w // LANES, bsz, nb),
        in_specs=[pl.BlockSpec((1, LANES), lambda p, b, i: (0, p)),
                  pl.BlockSpec(msk.shape, lambda p, b, i: (0, 0, 0)),
                  pl.BlockSpec(cum.shape, lambda p, b, i: (0, 0, 0))]
                 + [spec_f] * 6 + [spec_b] * 6,
        out_specs=[spec_f, spec_b],
        out_shape=[out, out],
        scratch_shapes=[pltpu.VMEM((LANES, LANES), F32), pltpu.VMEM((LANES, LANES), F32),
                        pltpu.VMEM((2 * nch, LANES, LANES), BF16),
                        pltpu.VMEM((2 * nch, LANES, LANES), F32),
                        pltpu.VMEM((2 * nch, LANES, LANES), BF16),
                        pltpu.VMEM((2 * nch, RW_CHUNK, LANES), F32),
                        pltpu.VMEM((2 * nch, 1, LANES), F32)],
        compiler_params=pltpu.CompilerParams(
            dimension_semantics=("parallel", "parallel", "arbitrary"),
            vmem_limit_bytes=VMEM_LIMIT),
        name="rwkv7_scan",
    )(k_a.reshape(1, w), msk, cum, r, k, v, kk, lw[0], a[0], r, k, v, kk, lw[1], a[1])


def _ssd_masks():
    i = np.arange(SSD_CHUNK)
    fwd = i[None, :] <= i[:, None]
    bwd = i[None, :] >= i[:, None]
    return np.stack([fwd, fwd.T, bwd, bwd.T]).astype(np.float32)


def _ssd_local(ci, refs, row0, reverse, msk_ref, y_ref, scr):
    new_s, ecol_s, etot_s, cc_s = scr
    dtm_ref, adtm_ref, adth_ref, x_ref, b_ref, c_ref = refs
    c = SSD_CHUNK
    d = 2 if reverse else 0
    rows = pl.ds(row0, c)
    dt_tm = dtm_ref[0, 0, 0, rows, :]
    adt_tm = adtm_ref[0, 0, 0, rows, :]
    adt_hm = adth_ref[0, 0, 0, :, rows]
    bc = b_ref[0, rows, :].astype(BF16)
    cc = c_ref[0, rows, :].astype(BF16)
    cc_s[ci] = cc
    mask = msk_ref[d]

    acum_tm = _dotf(mask, adt_tm, exact=True)
    acum_hm = _dotf(adt_hm, msk_ref[d + 1], exact=True)
    cb = _dot_nt(cc, bc)
    yield
    tot = acum_hm[:, 0:1] if reverse else acum_hm[:, c - 1:c]
    bt = bc.T
    lane = lax.broadcasted_iota(jnp.int32, (c, LANES), 1)
    first = lane < MB_HEAD
    lane1 = lax.broadcasted_iota(jnp.int32, (1, LANES), 1) < MB_HEAD

    for q in range(x_ref.shape[2] // LANES):
        e1, e2 = 2 * q, 2 * q + 1
        cols = slice(q * LANES, (q + 1) * LANES)
        col1 = jnp.broadcast_to(acum_tm[:, e1:e1 + 1], (c, LANES))
        col2 = jnp.broadcast_to(acum_tm[:, e2:e2 + 1], (c, LANES))
        colp = jnp.where(first, col1, col2)
        dtp = jnp.where(first, jnp.broadcast_to(dt_tm[:, e1:e1 + 1], (c, LANES)),
                        jnp.broadcast_to(dt_tm[:, e2:e2 + 1], (c, LANES)))
        totp = jnp.where(lane1, jnp.broadcast_to(tot[e1:e1 + 1, :], (1, LANES)),
                         jnp.broadcast_to(tot[e2:e2 + 1, :], (1, LANES)))
        xp = x_ref[0, rows, cols] * dtp
        xpb = xp.astype(BF16)
        l1 = jnp.exp(jnp.minimum(col1 - acum_hm[e1:e1 + 1, :], 0.0)) * mask
        l2 = jnp.exp(jnp.minimum(col2 - acum_hm[e2:e2 + 1, :], 0.0)) * mask
        yd1 = _dotf(l1 * cb, xpb)
        yd2 = _dotf(l2 * cb, xpb)
        ns = _dotf(bt, jnp.exp(totp - colp) * xp)
        ecol_s[ci, q] = jnp.exp(colp)
        etot_s[ci, q] = jnp.exp(totp)
        yield
        y_ref[0, rows, cols] = jnp.where(first, yd1, yd2)
        new_s[ci, q] = ns


def _ssd_state_step(ci, y_ref, row0, s_ref, scr):
    new_s, ecol_s, etot_s, cc_s = scr
    rows = pl.ds(row0, SSD_CHUNK)
    for q in range(s_ref.shape[0]):
        cols = slice(q * LANES, (q + 1) * LANES)
        st = s_ref[q]
        y_ref[0, rows, cols] += ecol_s[ci, q] * _dotf(cc_s[ci], st)
        s_ref[q] = etot_s[ci, q] * st + new_s[ci, q]


def _ssd_kernel(msk_ref, *refs, n_chunks):
    fwd, bwd = refs[0:6], refs[6:12]
    yf_ref, yb_ref, sf_ref, sb_ref = refs[12:16]
    scr = refs[16:20]

    @pl.when(pl.program_id(2) == 0)
    def _():
        sf_ref[...] = jnp.zeros_like(sf_ref)
        sb_ref[...] = jnp.zeros_like(sb_ref)

    stages = []
    for j in range(n_chunks):
        stages.append(_ssd_local(j, fwd, j * SSD_CHUNK, False, msk_ref, yf_ref, scr))
        stages.append(_ssd_local(n_chunks + j, bwd, j * SSD_CHUNK, True, msk_ref, yb_ref, scr))
    _lockstep(stages)
    for j in range(n_chunks):
        _ssd_state_step(j, yf_ref, j * SSD_CHUNK, sf_ref, scr)
        jb = n_chunks - 1 - j
        _ssd_state_step(n_chunks + jb, yb_ref, jb * SSD_CHUNK, sb_ref, scr)


def _ssd_scan(xbc, dt, a_neg, *, tb=512):
    bsz, t, _ = xbc.shape
    g = MB_GROUPS
    nh = dt.shape[-1]
    e = nh // g
    w = nh * MB_HEAD
    gw = e * MB_HEAD
    tb = min(tb, t)
    nb = t // tb
    assert t % tb == 0 and tb % SSD_CHUNK == 0 and MB_STATE == LANES
    dt_g = dt.reshape(bsz, t, 2, g, e)
    adt_g = dt_g * a_neg.reshape(2, g, e)
    dt_tm = jnp.transpose(dt_g, (2, 0, 3, 1, 4))
    adt_tm = jnp.transpose(adt_g, (2, 0, 3, 1, 4))
    adt_hm = jnp.transpose(adt_g, (2, 0, 3, 4, 1))

    def specs(d, tmap):
        return [
            pl.BlockSpec((1, 1, 1, tb, e), lambda b, gi, i: (d, b, gi, tmap(i), 0)),
            pl.BlockSpec((1, 1, 1, tb, e), lambda b, gi, i: (d, b, gi, tmap(i), 0)),
            pl.BlockSpec((1, 1, 1, e, tb), lambda b, gi, i: (d, b, gi, 0, tmap(i))),
            pl.BlockSpec((1, tb, gw), lambda b, gi, i: (b, tmap(i), gi)),
            pl.BlockSpec((1, tb, MB_STATE), lambda b, gi, i: (b, tmap(i), w // MB_STATE + gi)),
            pl.BlockSpec((1, tb, MB_STATE), lambda b, gi, i: (b, tmap(i), w // MB_STATE + g + gi)),
        ]

    fmap = lambda i: i
    bmap = lambda i: nb - 1 - i
    out = jax.ShapeDtypeStruct((bsz, t, w), F32)
    n_pairs = gw // LANES
    nch = tb // SSD_CHUNK
    msk = jnp.asarray(_ssd_masks())
    return pl.pallas_call(
        functools.partial(_ssd_kernel, n_chunks=nch),
        grid=(bsz, g, nb),
        in_specs=[pl.BlockSpec(msk.shape, lambda b, gi, i: (0, 0, 0))] + specs(0, fmap) + specs(1, bmap),
        out_specs=[pl.BlockSpec((1, tb, gw), lambda b, gi, i: (b, i, gi)),
                   pl.BlockSpec((1, tb, gw), lambda b, gi, i: (b, nb - 1 - i, gi))],
        out_shape=[out, out],
        scratch_shapes=[pltpu.VMEM((n_pairs, MB_STATE, LANES), F32),
                        pltpu.VMEM((n_pairs, MB_STATE, LANES), F32),
                        pltpu.VMEM((2 * nch, n_pairs, MB_STATE, LANES), F32),
                        pltpu.VMEM((2 * nch, n_pairs, SSD_CHUNK, LANES), F32),
                        pltpu.VMEM((2 * nch, n_pairs, 1, LANES), F32),
                        pltpu.VMEM((2 * nch, SSD_CHUNK, MB_STATE), BF16)],
        compiler_params=pltpu.CompilerParams(
            dimension_semantics=("parallel", "parallel", "arbitrary"),
            vmem_limit_bytes=VMEM_LIMIT),
        name="ssd_scan",
    )(msk, dt_tm, adt_tm, adt_hm, xbc, xbc, xbc, dt_tm, adt_tm, adt_hm, xbc, xbc, xbc)


def _layer_norm(x, g, b):
    mu = jnp.mean(x, -1, keepdims=True)
    xc = x - mu
    var = jnp.mean(xc * xc, -1, keepdims=True)
    return xc * lax.rsqrt(var + LN_EPS) * g + b


def _pad_cols(wm, n):
    return jnp.pad(wm, ((0, 0), (0, n - wm.shape[1])))


def _pad_rows(wm, n):
    return jnp.pad(wm, ((0, n - wm.shape[0]), (0, 0)))


def _stack_rows(vecs, rows=8):
    a = jnp.stack([v.reshape(-1).astype(F32) for v in vecs])
    return _pad_rows(a, -(-a.shape[0] // rows) * rows)


HALO = 16


def _halo_specs(tm, tn, n_rows, col_off=0):
    last = n_rows // HALO - 1
    rt = tm // HALO
    return [pl.BlockSpec((tm, tn), lambda i, j: (i, col_off + j)),
            pl.BlockSpec((HALO, tn), lambda i, j: (jnp.maximum(i * rt - 1, 0), col_off + j)),
            pl.BlockSpec((HALO, tn), lambda i, j: (jnp.minimum((i + 1) * rt, last), col_off + j))]


def _stage_rows(ext_ref, x_ref, prev_ref, next_ref, tiles_per_seq):
    tm = x_ref.shape[0]
    i = pl.program_id(0)
    first = (i % tiles_per_seq) == 0
    last = (i % tiles_per_seq) == tiles_per_seq - 1
    ext_ref[0:HALO, :] = jnp.where(first, 0.0, prev_ref[...].astype(F32))
    ext_ref[HALO:HALO + tm, :] = x_ref[...].astype(F32)
    ext_ref[HALO + tm:2 * HALO + tm, :] = jnp.where(last, 0.0, next_ref[...].astype(F32))


def _shifted(ext_ref, s, tm):
    return ext_ref[pl.ds(HALO + s, tm), :]


def _delta(ext_ref, tm):
    return 0.5 * (_shifted(ext_ref, -1, tm) + _shifted(ext_ref, 1, tm)) - _shifted(ext_ref, 0, tm)


def _softplus(x):
    return jnp.maximum(x, 0.0) + jnp.log(1.0 + jnp.exp(-jnp.abs(x)))


def _group_sum(x, e_ref):
    e = e_ref[...]
    outs = []
    for s in range(x.shape[1] // LANES):
        xs = x[:, s * LANES:(s + 1) * LANES]
        hi = xs.astype(BF16)
        lo = (xs - hi.astype(F32)).astype(BF16)
        outs.append(jnp.dot(hi, e, preferred_element_type=F32)
                    + jnp.dot(lo, e, preferred_element_type=F32))
    return outs[0] if len(outs) == 1 else jnp.concatenate(outs, axis=1)


def _head_indicator():
    i = np.arange(LANES) // RW_HEAD
    return jnp.asarray((i[:, None] == i[None, :]).astype(np.float32), dtype=BF16)


def _row_params(tm, seq_len):
    tm = min(tm, seq_len)
    assert seq_len % tm == 0 and tm % HALO == 0
    return tm, seq_len // tm


def _lora_in_kernel(h_ref, hp_ref, hn_ref, mu_ref, w_ref, o_ref, ext_ref, *, tiles_per_seq, n_lora):
    tm = h_ref.shape[0]
    _stage_rows(ext_ref, h_ref, hp_ref, hn_ref, tiles_per_seq)
    h = h_ref[...]
    hd = _delta(ext_ref, tm)
    src = (0, 0, 1, 1, 2)
    for i in range(n_lora):
        x = (h + hd * mu_ref[src[i]:src[i] + 1, :]).astype(BF16)
        mid = jnp.dot(x, w_ref[:, i * LANES:(i + 1) * LANES], preferred_element_type=F32)
        if i < 2:
            mid = jnp.tanh(mid)
        o_ref[:, i * LANES:(i + 1) * LANES] = mid.astype(o_ref.dtype)


def _lora_in(h, mus, w_first, seq_len, *, tm=512):
    m, d = h.shape
    tm, tps = _row_params(tm, seq_len)
    n_lora = len(w_first)
    w = jnp.concatenate([_pad_cols(x, LANES) for x in w_first], axis=1).astype(BF16)
    mu = _stack_rows(mus)
    return pl.pallas_call(
        functools.partial(_lora_in_kernel, tiles_per_seq=tps, n_lora=n_lora),
        grid=(m // tm, 1),
        in_specs=_halo_specs(tm, d, m) + [pl.BlockSpec(mu.shape, lambda i, j: (0, 0)),
                                          pl.BlockSpec(w.shape, lambda i, j: (0, 0))],
        out_specs=pl.BlockSpec((tm, n_lora * LANES), lambda i, j: (i, 0)),
        out_shape=jax.ShapeDtypeStruct((m, n_lora * LANES), BF16),
        scratch_shapes=[pltpu.VMEM((tm + 2 * HALO, d), F32)],
        compiler_params=pltpu.CompilerParams(
            dimension_semantics=("parallel", "arbitrary"), vmem_limit_bytes=VMEM_LIMIT),
        name="lora_in",
    )(h, h, h, mu, w)


_P_MU_R, _P_MU_K, _P_MU_V, _P_W0, _P_A0, _P_KK, _P_KA, _P_RK, _P_V0 = 0, 1, 2, 3, 5, 7, 8, 9, 10


def _rwkv_prep_kernel(*refs, tiles_per_seq, has_vres):
    (pr, prp, prn, pk, pkp, pkn, pv, pvp, pvn, mid_ref, w2_ref, par_ref, e_ref) = refs[:13]
    pos = 13
    vf_ref = None
    if has_vres:
        vf_ref = refs[pos]
        pos += 1
    r_o, k_o, v_o, kk_o, lw0_o, lw1_o, a0_o, a1_o, bonus_o = refs[pos:pos + 9]
    er, ek, ev = refs[pos + 9:pos + 12]
    tm = pr.shape[0]
    par = lambda i: par_ref[i:i + 1, :]

    outs = []
    for ext, x, xp, xn, mu in ((er, pr, prp, prn, _P_MU_R), (ek, pk, pkp, pkn, _P_MU_K),
                               (ev, pv, pvp, pvn, _P_MU_V)):
        _stage_rows(ext, x, xp, xn, tiles_per_seq)
        outs.append(x[...] + par(mu) * _delta(ext, tm))
    r, k, v = outs

    def lora(i):
        return jnp.dot(mid_ref[:, i * LANES:(i + 1) * LANES], w2_ref[i],
                       preferred_element_type=F32)

    kdir = 0.0
    for d, (lw_o, a_o) in enumerate(((lw0_o, a0_o), (lw1_o, a1_o))):
        w_log = -_softplus(-(par(_P_W0 + d) + lora(d))) - 0.5
        lw_o[...] = -jnp.exp(w_log)
        a = _sigmoid(par(_P_A0 + d) + lora(2 + d))
        a_o[...] = a.astype(a_o.dtype)
        kdir = kdir + (1.0 + (a - 1.0) * par(_P_KA))
    if has_vres:
        vg = _sigmoid(par(_P_V0) + lora(4))
        v = v + (vf_ref[...].astype(F32) - v) * vg
    kk = k * par(_P_KK)
    nrm = jnp.maximum(jnp.sqrt(_group_sum(kk * kk, e_ref)), 1e-12)
    r_o[...] = r.astype(r_o.dtype)
    k_o[...] = k.astype(k_o.dtype)
    v_o[...] = v.astype(v_o.dtype)
    kk_o[...] = (kk / nrm).astype(kk_o.dtype)
    bonus_o[...] = (_group_sum(r * k * kdir * par(_P_RK), e_ref) * v).astype(bonus_o.dtype)


def _rwkv_prep(p_rkv, mid, w_second, params, v_first, seq_len, *, tm=512, tn=256):
    m, w3 = p_rkv.shape
    w = w3 // 3
    tm, tps = _row_params(tm, seq_len)
    tn = min(tn, w)
    nj = w // tn
    assert w % tn == 0 and tn % LANES == 0
    has_vres = v_first is not None
    n_lora = len(w_second)
    w2 = jnp.stack([_pad_rows(x, LANES) for x in w_second]).astype(BF16)
    par = _stack_rows(params)
    tile = pl.BlockSpec((tm, tn), lambda i, j: (i, j))
    in_specs = (_halo_specs(tm, tn, m, 0) + _halo_specs(tm, tn, m, nj) + _halo_specs(tm, tn, m, 2 * nj)
                + [pl.BlockSpec((tm, mid.shape[1]), lambda i, j: (i, 0)),
                   pl.BlockSpec((n_lora, LANES, tn), lambda i, j: (0, 0, j)),
                   pl.BlockSpec((par.shape[0], tn), lambda i, j: (0, j)),
                   pl.BlockSpec((LANES, LANES), lambda i, j: (0, 0))])
    args = [p_rkv] * 9 + [mid, w2, par, _head_indicator()]
    if has_vres:
        in_specs.append(tile)
        args.append(v_first)
    half = jax.ShapeDtypeStruct((m, w), BF16)
    full = jax.ShapeDtypeStruct((m, w), F32)
    return pl.pallas_call(
        functools.partial(_rwkv_prep_kernel, tiles_per_seq=tps, has_vres=has_vres),
        grid=(m // tm, nj),
        in_specs=in_specs,
        out_specs=[tile] * 9,
        out_shape=[half] * 4 + [full] * 2 + [half] * 3,
        scratch_shapes=[pltpu.VMEM((tm + 2 * HALO, tn), F32)] * 3,
        compiler_params=pltpu.CompilerParams(
            dimension_semantics=("parallel", "parallel"), vmem_limit_bytes=VMEM_LIMIT),
        name="rwkv_prep",
    )(*args)


def _rwkv_post_kernel(yf, yb, bonus, gate, par_ref, e_ref, o_ref):
    lnx_g, lnx_b = par_ref[0:1, :], par_ref[1:2, :]
    inv_n = 1.0 / RW_HEAD
    y = yf[...] + yb[...]
    yc = y - _group_sum(y, e_ref) * inv_n
    var = _group_sum(yc * yc, e_ref) * inv_n
    gn = yc * lax.rsqrt(var + RW_GN_EPS) * lnx_g + lnx_b
    o_ref[...] = ((gn + bonus[...].astype(F32)) * gate[...].astype(F32)).astype(o_ref.dtype)


def _rwkv_post(y_f, y_b, bonus, gate, lnx_g, lnx_b, *, tm=512, tn=512):
    m, w = y_f.shape
    tm = min(tm, m)
    tn = min(tn, w)
    par = _stack_rows([lnx_g, lnx_b])
    tile = pl.BlockSpec((tm, tn), lambda i, j: (i, j))
    return pl.pallas_call(
        _rwkv_post_kernel,
        grid=(m // tm, w // tn),
        in_specs=[tile] * 4 + [pl.BlockSpec((par.shape[0], tn), lambda i, j: (0, j)),
                               pl.BlockSpec((LANES, LANES), lambda i, j: (0, 0))],
        out_specs=tile,
        out_shape=jax.ShapeDtypeStruct((m, w), BF16),
        compiler_params=pltpu.CompilerParams(
            dimension_semantics=("parallel", "parallel"), vmem_limit_bytes=VMEM_LIMIT),
        name="rwkv_post",
    )(y_f, y_b, bonus, gate, par, _head_indicator())


CONV_BLK = 128
CONV_PAD = 64


def _conv_shift_matrix():
    taps = [j for j in range(CONV_W) if j != CONV_W // 2]
    s = np.zeros((len(taps) * CONV_BLK, 2 * CONV_BLK), np.float32)
    for i, j in enumerate(taps):
        t = np.arange(CONV_BLK)
        s[i * CONV_BLK + t, CONV_PAD + t + j - CONV_W // 2] = 1.0
    return taps, s


def _conv_silu_kernel(x_ref, xp_ref, xn_ref, w_ref, s_ref, o_ref, ext_ref, *, tiles_per_seq):
    tm = x_ref.shape[0]
    i = pl.program_id(0)
    first = (i % tiles_per_seq) == 0
    last = (i % tiles_per_seq) == tiles_per_seq - 1
    zero = jnp.zeros((CONV_PAD - HALO, ext_ref.shape[1]), ext_ref.dtype)
    ext_ref[0:CONV_PAD - HALO, :] = zero
    halo0 = jnp.zeros(xp_ref.shape, xp_ref.dtype)
    ext_ref[CONV_PAD - HALO:CONV_PAD, :] = jnp.where(first, halo0, xp_ref[...])
    ext_ref[CONV_PAD:CONV_PAD + tm, :] = x_ref[...]
    ext_ref[CONV_PAD + tm:CONV_PAD + tm + HALO, :] = jnp.where(last, halo0, xn_ref[...])
    ext_ref[CONV_PAD + tm + HALO:2 * CONV_PAD + tm, :] = zero
    taps, _ = _conv_shift_matrix()
    mid = CONV_W // 2
    for b in range(tm // CONV_BLK):
        rows = slice(b * CONV_BLK, (b + 1) * CONV_BLK)
        src = ext_ref[b * CONV_BLK:(b + 2) * CONV_BLK, :]
        sh = jnp.dot(s_ref[...], src, preferred_element_type=F32)
        acc = w_ref[CONV_W:CONV_W + 1, :] + x_ref[rows, :].astype(F32) * w_ref[mid:mid + 1, :]
        for n, j in enumerate(taps):
            acc = acc + sh[n * CONV_BLK:(n + 1) * CONV_BLK] * w_ref[j:j + 1, :]
        o_ref[rows, :] = (acc * _sigmoid(acc)).astype(o_ref.dtype)


def _conv_silu(x, conv_w, conv_b, seq_len, *, tm=512, tn=512):
    m, ch = x.shape
    tm, tps = _row_params(tm, seq_len)
    tn = min(tn, ch)
    assert ch % tn == 0 and CONV_W // 2 <= HALO <= CONV_PAD and tm % CONV_BLK == 0
    wb = jnp.concatenate([conv_w, conv_b[None, :]], axis=0)
    shift = jnp.asarray(_conv_shift_matrix()[1], dtype=x.dtype)
    return pl.pallas_call(
        functools.partial(_conv_silu_kernel, tiles_per_seq=tps),
        grid=(m // tm, ch // tn),
        in_specs=_halo_specs(tm, tn, m) + [pl.BlockSpec((CONV_W + 1, tn), lambda i, j: (0, j)),
                                           pl.BlockSpec(shift.shape, lambda i, j: (0, 0))],
        out_specs=pl.BlockSpec((tm, tn), lambda i, j: (i, j)),
        out_shape=jax.ShapeDtypeStruct((m, ch), BF16),
        scratch_shapes=[pltpu.VMEM((tm + 2 * CONV_PAD, tn), x.dtype)],
        compiler_params=pltpu.CompilerParams(
            dimension_semantics=("parallel", "parallel"), vmem_limit_bytes=VMEM_LIMIT),
        name="conv_silu",
    )(x, x, x, wb, shift)


def _mamba_post_kernel(yf, yb, xm, gate, par_ref, o_ref):
    d_skip, norm_g = par_ref[0:1, :], par_ref[1:2, :]
    y = (yf[...] + yb[...] + d_skip * xm[...].astype(F32)) * gate[...].astype(F32)
    ms = jnp.mean(y * y, axis=-1, keepdims=True)
    o_ref[...] = (y * lax.rsqrt(ms + RMS_EPS) * norm_g).astype(o_ref.dtype)


def _mamba_post(ys_f, ys_b, xbc_c, z, d_skip_rep, norm_g, *, tm=512):
    m, w = z.shape
    tm = min(tm, m)
    gw = w // MB_GROUPS
    par = _stack_rows([d_skip_rep, norm_g])
    tile = pl.BlockSpec((tm, gw), lambda i, j: (i, j))
    return pl.pallas_call(
        _mamba_post_kernel,
        grid=(m // tm, MB_GROUPS),
        in_specs=[tile] * 4 + [pl.BlockSpec((par.shape[0], gw), lambda i, j: (0, j))],
        out_specs=tile,
        out_shape=jax.ShapeDtypeStruct((m, w), BF16),
        compiler_params=pltpu.CompilerParams(
            dimension_semantics=("parallel", "parallel"), vmem_limit_bytes=VMEM_LIMIT),
        name="mamba_post",
    )(ys_f, ys_b, xbc_c, z, par)


def _merge_kernel(x_ref, w_ref, pr_ref, ga_ref, gb_ref, o_ref):
    pm = jnp.dot(x_ref[...], w_ref[...], preferred_element_type=F32)
    o_ref[...] = (ga_ref[...].astype(F32) * pr_ref[...] + gb_ref[...].astype(F32) * pm).astype(o_ref.dtype)


def _merge_matmul(o_mb, p_mb, pr, gates, *, tm=512, tn=512):
    m, k = o_mb.shape
    n = p_mb.shape[1]
    tm = min(tm, m)
    tn = min(tn, n)
    nj = n // tn
    tile = pl.BlockSpec((tm, tn), lambda i, j: (i, j))
    return pl.pallas_call(
        _merge_kernel,
        grid=(m // tm, nj),
        in_specs=[pl.BlockSpec((tm, k), lambda i, j: (i, 0)),
                  pl.BlockSpec((k, tn), lambda i, j: (0, j)),
                  tile, tile, pl.BlockSpec((tm, tn), lambda i, j: (i, nj + j))],
        out_specs=tile,
        out_shape=jax.ShapeDtypeStruct((m, n), BF16),
        compiler_params=pltpu.CompilerParams(
            dimension_semantics=("parallel", "parallel"), vmem_limit_bytes=VMEM_LIMIT),
        name="merge_matmul",
    )(o_mb, p_mb, pr, gates, gates)


def _out_ln_kernel(x_ref, w_ref, h_ref, par_ref, o_ref, ob_ref):
    out = jnp.dot(x_ref[...], w_ref[...], preferred_element_type=F32)
    res = ALPHA * h_ref[...] + out
    mu = jnp.mean(res, axis=-1, keepdims=True)
    xc = res - mu
    var = jnp.mean(xc * xc, axis=-1, keepdims=True)
    hn = xc * lax.rsqrt(var + LN_EPS) * par_ref[0:1, :] + par_ref[1:2, :]
    o_ref[...] = hn
    ob_ref[...] = hn.astype(ob_ref.dtype)


def _out_ln(mix, w_out, h, ln_g, ln_b, *, tm=256):
    m, k = mix.shape
    n = w_out.shape[1]
    tm = min(tm, m)
    par = _stack_rows([ln_g, ln_b])
    return pl.pallas_call(
        _out_ln_kernel,
        grid=(m // tm,),
        in_specs=[pl.BlockSpec((tm, k), lambda i: (i, 0)),
                  pl.BlockSpec((k, n), lambda i: (0, 0)),
                  pl.BlockSpec((tm, n), lambda i: (i, 0)),
                  pl.BlockSpec(par.shape, lambda i: (0, 0))],
        out_specs=[pl.BlockSpec((tm, n), lambda i: (i, 0))] * 2,
        out_shape=[jax.ShapeDtypeStruct((m, n), F32), jax.ShapeDtypeStruct((m, n), BF16)],
        compiler_params=pltpu.CompilerParams(
            dimension_semantics=("parallel",), vmem_limit_bytes=VMEM_LIMIT),
        name="out_ln",
    )(mix, w_out, h, par)


def _trunk(x, ln_in_g, ln_in_b, w_in, mu_rkv, mu_wa, w0, w1, w2, a0, a1, a2, mu_v, v0, v1, v2,
           k_k, k_a, r_k, lnx_g, lnx_b, p_rw, conv_w, conv_b, dt_bias, a_log, d_skip, mb_norm_g,
           p_mb, w_out, ln_g, ln_b):
    bsz, t, d = x.shape
    m = bsz * t
    rw_w = d
    mb_w = 2 * d
    rw_heads = rw_w // RW_HEAD
    mb_heads = mb_w // MB_HEAD
    conv_ch = mb_w + 2 * MB_GROUPS * MB_STATE
    splits = [3 * rw_w, 4 * rw_w, 4 * rw_w + mb_w, 4 * rw_w + mb_w + conv_ch,
              4 * rw_w + mb_w + conv_ch + 2 * mb_heads]
    seq = lambda a: a.reshape(bsz, t, a.shape[-1])

    h = _layer_norm(x, ln_in_g, ln_in_b).reshape(m, d)
    hb = h.astype(BF16)
    v_first = None
    for l in range(DEPTH):
        wl = w_in[l].astype(BF16)
        bounds = [0] + splits + [wl.shape[1]]
        outs = ((F32, None), (BF16, "silu"), (BF16, "silu"), (BF16, None), (F32, None), (BF16, "sigmoid"))
        p_rkv, g_rw, z, xbc, dt_raw, gates = [
            _matmul(hb, wl[:, bounds[i]:bounds[i + 1]], out_dtype=outs[i][0], act=outs[i][1])
            for i in range(6)]

        mus = [mu_wa[l, 0], mu_wa[l, 1]]
        w_first = [w1[l, 0], w1[l, 1], a1[l, 0], a1[l, 1]]
        w_second = [w2[l, 0], w2[l, 1], a2[l, 0], a2[l, 1]]
        params = [mu_rkv[l, 0], mu_rkv[l, 1], mu_rkv[l, 2], w0[l, 0], w0[l, 1], a0[l, 0], a0[l, 1],
                  k_k[l], k_a[l], r_k[l]]
        if l > 0:
            mus.append(mu_v[l - 1])
            w_first.append(v1[l - 1])
            w_second.append(v2[l - 1])
            params.append(v0[l - 1])
        mid = _lora_in(h, mus, w_first, t)
        r, k, v, kk, lw0, lw1, a_0, a_1, bonus = _rwkv_prep(p_rkv, mid, w_second, params, v_first, t)
        if l == 0:
            v_first = v
        y_f, y_b = _rwkv_scan(seq(r), seq(k), seq(v), seq(kk), (seq(lw0), seq(lw1)),
                              (seq(a_0), seq(a_1)), k_a[l])
        o_rw = _rwkv_post(y_f.reshape(m, rw_w), y_b.reshape(m, rw_w), bonus, g_rw, lnx_g[l], lnx_b[l])

        xbc_c = _conv_silu(xbc, conv_w[l], conv_b[l], t)
        dt = jax.nn.softplus(dt_raw.reshape(bsz, t, 2, mb_heads) + dt_bias[l])
        a_neg = -jnp.exp(a_log[l])
        ys_f, ys_b = _ssd_scan(seq(xbc_c), dt, a_neg)
        o_mb = _mamba_post(ys_f.reshape(m, mb_w), ys_b.reshape(m, mb_w), xbc_c, z,
                           jnp.repeat(d_skip[l], MB_HEAD), mb_norm_g[l])

        pr = _matmul(o_rw, p_rw[l].astype(BF16))
        mix = _merge_matmul(o_mb, p_mb[l].astype(BF16), pr, gates)
        h, hb = _out_ln(mix, w_out[l].astype(BF16), h, ln_g[l], ln_b[l])
    return h.reshape(bsz, t, d)


def kernel(x_prompt, x_sample, ln_in_g, ln_in_b, w_in, mu_rkv, mu_wa, w0, w1, w2, a0, a1, a2, mu_v, v0, v1, v2, k_k, k_a, r_k, lnx_g, lnx_b, p_rw, conv_w, conv_b, dt_bias, a_log, d_skip, mb_norm_g, p_mb, w_out, ln_g, ln_b):
    assert x_prompt.shape[1:] == x_sample.shape[1:]
    nb = x_prompt.shape[0]
    x = jnp.concatenate([x_prompt, x_sample], axis=0)
    y = _trunk(x, ln_in_g, ln_in_b, w_in, mu_rkv, mu_wa, w0, w1, w2, a0, a1, a2, mu_v, v0, v1, v2,
               k_k, k_a, r_k, lnx_g, lnx_b, p_rw, conv_w, conv_b, dt_bias, a_log, d_skip, mb_norm_g,
               p_mb, w_out, ln_g, ln_b)
    y = y.astype(x_prompt.dtype)
    return (y[:nb], y[nb:])
```

```python
import functools
import math

import numpy as np
import jax
import jax.numpy as jnp
from jax import lax
from jax.experimental import pallas as pl
from jax.experimental.pallas import tpu as pltpu

F32 = jnp.float32
BF16 = jnp.bfloat16

DEPTH = 2
RW_HEAD = 64
RW_GN_EPS = 64e-5
MB_HEAD = 64
MB_GROUPS = 8
MB_STATE = 128
CONV_W = 7
ALPHA = (2 * DEPTH) ** 0.25
LN_EPS = 1e-5
RMS_EPS = 1e-5

LANES = 128
RW_CHUNK = 64
SSD_CHUNK = 128
VMEM_LIMIT = 56 * 1024 * 1024


def _sigmoid(x):
    return 1.0 / (1.0 + jnp.exp(-x))


def _mm_kernel(x_ref, w_ref, o_ref, *, act):
    acc = jnp.dot(x_ref[...], w_ref[...], preferred_element_type=F32)
    if act == "sigmoid":
        acc = _sigmoid(acc)
    elif act == "silu":
        acc = acc * _sigmoid(acc)
    o_ref[...] = acc.astype(o_ref.dtype)


def _matmul(x, w, *, tm=1024, tn=1024, out_dtype=F32, act=None):
    m, k = x.shape
    n = w.shape[1]
    tm = min(tm, m)
    tn = min(tn, n)
    while n % tn:
        tn //= 2
    assert m % tm == 0 and (tn % LANES == 0 or tn == n)
    return pl.pallas_call(
        functools.partial(_mm_kernel, act=act),
        grid=(m // tm, n // tn),
        in_specs=[pl.BlockSpec((tm, k), lambda i, j: (i, 0)),
                  pl.BlockSpec((k, tn), lambda i, j: (0, j))],
        out_specs=pl.BlockSpec((tm, tn), lambda i, j: (i, j)),
        out_shape=jax.ShapeDtypeStruct((m, n), out_dtype),
        compiler_params=pltpu.CompilerParams(
            dimension_semantics=("parallel", "parallel"),
            vmem_limit_bytes=VMEM_LIMIT),
        name="dense_matmul",
    )(x, w)


def _dotf(a, b, exact=False):
    if exact:
        return jnp.dot(a, b, preferred_element_type=F32,
                       precision=lax.Precision.HIGHEST)
    return jnp.dot(a.astype(BF16), b.astype(BF16), preferred_element_type=F32)


def _dot_nt(a, b):
    return lax.dot_general(a.astype(BF16), b.astype(BF16),
                           (((1,), (1,)), ((), ())), preferred_element_type=F32)


def _dot_tn(a, b):
    return lax.dot_general(a.astype(BF16), b.astype(BF16),
                           (((0,), (0,)), ((), ())), preferred_element_type=F32)


def _dot_split(ones, x, terms):
    ob = ones.astype(BF16)
    acc = None
    for _ in range(terms):
        part = x.astype(BF16)
        x = x - part.astype(F32)
        term = jnp.dot(ob, part, preferred_element_type=F32)
        acc = term if acc is None else acc + term
    return acc


def _lockstep(stages):
    while stages:
        stages = [g for g in stages if next(g, StopIteration) is not StopIteration]


_M_SAME, _M_EYE, _M_STRICT, _M_INCL, _M_BASE, _M_LEVEL = 0, 1, 2, 3, 6, 7


def _rwkv_masks():
    c = RW_CHUNK
    i = np.arange(LANES)
    row, col = i[:, None], i[None, :]
    same = (row >= c) == (col >= c)
    tl, ts = row % c, col % c
    out = [same, row == col,
           same & (tl > ts), same & (tl >= ts),
           same & (tl < ts), same & (tl <= ts),
           same & ((tl // 2) == (ts // 2))]
    s = 2
    while s < c:
        out.append(same & ((tl // (2 * s)) == (ts // (2 * s))) & ((tl // s) != (ts // s)))
        s *= 2
    return np.stack(out).astype(np.float32)


def _rwkv_cum():
    i = np.arange(RW_CHUNK)
    return np.stack([i[None, :] <= i[:, None], i[None, :] >= i[:, None]]).astype(np.float32)


def _rwkv_local(ci, refs, row0, reverse, ka, msk_ref, cum_ref, scr):
    mw_s, n_s, q_s, yl_s, dec_s = scr
    r_ref, k_ref, v_ref, kk_ref, lw_ref, a_ref = refs
    c = RW_CHUNK
    d = 1 if reverse else 0
    sl = (0, pl.ds(row0, c), slice(None))
    r = r_ref[sl].astype(F32)
    k = k_ref[sl].astype(F32)
    v = v_ref[sl].astype(F32)
    kk = kk_ref[sl].astype(F32)
    lw = lw_ref[sl]
    a = a_ref[sl].astype(F32)
    same = msk_ref[_M_SAME]
    strict = msk_ref[_M_STRICT + 2 * d]
    incl = msk_ref[_M_INCL + 2 * d]

    cl = _dot_split(cum_ref[d], lw, 2)
    yield
    ce = cl - lw
    ct = cl[0:1, :] if reverse else cl[c - 1:c, :]
    kd = k * (1.0 + (a - 1.0) * ka)
    bv = kk * a
    einv = jnp.exp(-cl)
    dec_end = jnp.exp(ct - cl)
    dec_s[ci] = jnp.exp(ct)

    def bd(x):
        return jnp.concatenate([x, x], axis=0) * same

    xa = bd(-kk * jnp.exp(ce)).astype(BF16)
    xr = bd(r * jnp.exp(cl))
    vb = bd(v).astype(BF16)
    bh = bd(bv * dec_end).astype(BF16)
    kh = bd(kd * dec_end).astype(BF16)

    gram = _dot_nt(jnp.concatenate([xa, xr.astype(BF16)], axis=0),
                   jnp.concatenate([bd(bv * einv), bd(kd * einv)], axis=0))
    yield
    a_ab = gram[:LANES, :LANES] * strict
    a_ak = gram[:LANES, LANES:] * strict
    a_rb = (gram[LANES:, :LANES] * incl).astype(BF16)
    a_rk = (gram[LANES:, LANES:] * incl).astype(BF16)

    inv = msk_ref[_M_EYE] + a_ab * msk_ref[_M_BASE]
    g = _dotf(a_ak, vb)
    yk = _dotf(a_rk, vb)
    for lvl in range(int(math.log2(c)) - 1):
        s = 2 << lvl
        off = (a_ab * msk_ref[_M_LEVEL + lvl]).astype(BF16)
        ib = inv.astype(BF16)
        if s < 8:
            t = _dotf(ib, off)
            yield
            t = _dotf(t, ib)
            yield
            inv = inv + t
        else:
            pieces = [inv[i * s:(i + 1) * s] for i in range(LANES // s)]
            sel = [i for i in range(LANES // s) if (i % 2 == 0) == reverse]
            t = _dotf(jnp.concatenate([pieces[i] for i in sel], axis=0), off)
            yield
            t = _dotf(t, ib)
            yield
            for n, i in enumerate(sel):
                pieces[i] = pieces[i] + t[n * s:(n + 1) * s]
            inv = jnp.concatenate(pieces, axis=0)

    wu = _dotf(inv, jnp.concatenate([xa, g.astype(BF16)], axis=1))
    yield
    wu = wu.astype(BF16)
    mw = _dot_tn(wu[:, :LANES], bh)
    nn = _dot_tn(jnp.concatenate([wu[:, LANES:], vb], axis=0),
                 jnp.concatenate([bh, kh], axis=0))
    qy = _dotf(a_rb, wu)
    yield
    mw_s[ci] = mw.astype(BF16)
    n_s[ci] = nn
    q_s[ci] = (xr + qy[:, :LANES]).astype(BF16)
    yl = qy[:, LANES:] + yk
    yl_s[ci] = yl[:c] + yl[c:]


def _rwkv_state_steps(chunks, y_ref, h_ref, scr):
    mw_s, n_s, q_s, yl_s, dec_s = scr
    c = RW_CHUNK
    for ci, row0 in chunks:
        h = h_ref[...]
        hb = h.astype(BF16)
        y = _dot_nt(q_s[ci], hb)
        hm = _dotf(hb, mw_s[ci])
        yield
        y_ref[0, pl.ds(row0, c), :] = (y[:c] + y[c:] + yl_s[ci]).astype(y_ref.dtype)
        h_ref[...] = h * dec_s[ci] + hm + n_s[ci]


def _rwkv_kernel(ka_ref, msk_ref, cum_ref, *refs, n_chunks):
    fwd, bwd = refs[0:6], refs[6:12]
    yf_ref, yb_ref, hf_ref, hb_ref = refs[12:16]
    scr = refs[16:21]

    @pl.when(pl.program_id(2) == 0)
    def _():
        hf_ref[...] = jnp.zeros_like(hf_ref)
        hb_ref[...] = jnp.zeros_like(hb_ref)

    ka = ka_ref[...]
    order_f = list(range(n_chunks))
    order_b = list(range(n_chunks - 1, -1, -1))
    half = (n_chunks + 1) // 2
    state = []
    for lo, hi in ((0, half), (half, n_chunks)):
        local = []
        for jf, jb in zip(order_f[lo:hi], order_b[lo:hi]):
            local.append(_rwkv_local(jf, fwd, jf * RW_CHUNK, False, ka, msk_ref, cum_ref, scr))
            local.append(_rwkv_local(n_chunks + jb, bwd, jb * RW_CHUNK, True, ka, msk_ref, cum_ref, scr))
        _lockstep(local + state)
        state = [
            _rwkv_state_steps([(j, j * RW_CHUNK) for j in order_f[lo:hi]], yf_ref, hf_ref, scr),
            _rwkv_state_steps([(n_chunks + j, j * RW_CHUNK) for j in order_b[lo:hi]], yb_ref, hb_ref, scr)]
    _lockstep(state)


def _rwkv_scan(r, k, v, kk, lw, a, k_a, *, tb=1024):
    bsz, t, w = r.shape
    tb = min(tb, t)
    nb = t // tb
    nch = tb // RW_CHUNK
    assert t % tb == 0 and tb % RW_CHUNK == 0 and w % LANES == 0
    blk = (1, tb, LANES)
    f_map = lambda p, b, i: (b, i, p)
    b_map = lambda p, b, i: (b, nb - 1 - i, p)
    spec_f = pl.BlockSpec(blk, f_map)
    spec_b = pl.BlockSpec(blk, b_map)
    msk = jnp.asarray(_rwkv_masks())
    cum = jnp.asarray(_rwkv_cum())
    out = jax.ShapeDtypeStruct((bsz, t, w), BF16)
    return pl.pallas_call(
        functools.partial(_rwkv_kernel, n_chunks=nch),
        grid=(w // LANES, bsz, nb),
        in_specs=[pl.BlockSpec((1, LANES), lambda p, b, i: (0, p)),
                  pl.BlockSpec(msk.shape, lambda p, b, i: (0, 0, 0)),
                  pl.BlockSpec(cum.shape, lambda p, b, i: (0, 0, 0))]
                 + [spec_f] * 6 + [spec_b] * 6,
        out_specs=[spec_f, spec_b],
        out_shape=[out, out],
        scratch_shapes=[pltpu.VMEM((LANES, LANES), F32), pltpu.VMEM((LANES, LANES), F32),
                        pltpu.VMEM((2 * nch, LANES, LANES), BF16),
                        pltpu.VMEM((2 * nch, LANES, LANES), F32),
                        pltpu.VMEM((2 * nch, LANES, LANES), BF16),
                        pltpu.VMEM((2 * nch, RW_CHUNK, LANES), F32),
                        pltpu.VMEM((2 * nch, 1, LANES), F32)],
        compiler_params=pltpu.CompilerParams(
            dimension_semantics=("parallel", "parallel", "arbitrary"),
            vmem_limit_bytes=VMEM_LIMIT),
        name="rwkv7_scan",
    )(k_a.reshape(1, w), msk, cum, r, k, v, kk, lw[0], a[0], r, k, v, kk, lw[1], a[1])


def _ssd_masks():
    i = np.arange(SSD_CHUNK)
    fwd = i[None, :] <= i[:, None]
    bwd = i[None, :] >= i[:, None]
    return np.stack([fwd, fwd.T, bwd, bwd.T]).astype(np.float32)


def _ssd_local(ci, refs, row0, reverse, msk_ref, scr):
    new_s, ecol_s, etot_s, cc_s, yd_s = scr
    dtm_ref, adtm_ref, adth_ref, x_ref, b_ref, c_ref = refs
    c = SSD_CHUNK
    d = 2 if reverse else 0
    rows = pl.ds(row0, c)
    dt_tm = dtm_ref[0, 0, 0, rows, :]
    adt_tm = adtm_ref[0, 0, 0, rows, :]
    adt_hm = adth_ref[0, 0, 0, :, rows]
    bc = b_ref[0, rows, :].astype(BF16)
    cc = c_ref[0, rows, :].astype(BF16)
    cc_s[ci] = cc
    mask = msk_ref[d]

    acum_tm = _dotf(mask, adt_tm, exact=True)
    acum_hm = _dotf(adt_hm, msk_ref[d + 1], exact=True)
    cb = _dot_nt(cc, bc)
    yield
    tot = acum_hm[:, 0:1] if reverse else acum_hm[:, c - 1:c]
    bt = bc.T
    lane = lax.broadcasted_iota(jnp.int32, (c, LANES), 1)
    first = lane < MB_HEAD
    lane1 = lax.broadcasted_iota(jnp.int32, (1, LANES), 1) < MB_HEAD

    for q in range(x_ref.shape[2] // LANES):
        e1, e2 = 2 * q, 2 * q + 1
        cols = slice(q * LANES, (q + 1) * LANES)
        col1 = jnp.broadcast_to(acum_tm[:, e1:e1 + 1], (c, LANES))
        col2 = jnp.broadcast_to(acum_tm[:, e2:e2 + 1], (c, LANES))
        colp = jnp.where(first, col1, col2)
        dtp = jnp.where(first, jnp.broadcast_to(dt_tm[:, e1:e1 + 1], (c, LANES)),
                        jnp.broadcast_to(dt_tm[:, e2:e2 + 1], (c, LANES)))
        totp = jnp.where(lane1, jnp.broadcast_to(tot[e1:e1 + 1, :], (1, LANES)),
                         jnp.broadcast_to(tot[e2:e2 + 1, :], (1, LANES)))
        xp = x_ref[0, rows, cols] * dtp
        xpb = xp.astype(BF16)
        l1 = jnp.exp(jnp.minimum(col1 - acum_hm[e1:e1 + 1, :], 0.0)) * mask
        l2 = jnp.exp(jnp.minimum(col2 - acum_hm[e2:e2 + 1, :], 0.0)) * mask
        yd1 = _dotf(l1 * cb, xpb)
        yd2 = _dotf(l2 * cb, xpb)
        ns = _dotf(bt, jnp.exp(totp - colp) * xp)
        ecol_s[ci, q] = jnp.exp(colp)
        etot_s[ci, q] = jnp.exp(totp)
        yield
        yd_s[ci, q] = jnp.where(first, yd1, yd2)
        new_s[ci, q] = ns


def _ssd_state_step(ci, y_ref, row0, s_ref, scr):
    new_s, ecol_s, etot_s, cc_s, yd_s = scr
    rows = pl.ds(row0, SSD_CHUNK)
    for q in range(s_ref.shape[0]):
        cols = slice(q * LANES, (q + 1) * LANES)
        st = s_ref[q]
        y = yd_s[ci, q] + ecol_s[ci, q] * _dotf(cc_s[ci], st)
        y_ref[0, rows, cols] = y.astype(y_ref.dtype)
        s_ref[q] = etot_s[ci, q] * st + new_s[ci, q]


def _ssd_kernel(msk_ref, *refs, n_chunks):
    fwd, bwd = refs[0:6], refs[6:12]
    yf_ref, yb_ref, sf_ref, sb_ref = refs[12:16]
    scr = refs[16:21]

    @pl.when(pl.program_id(2) == 0)
    def _():
        sf_ref[...] = jnp.zeros_like(sf_ref)
        sb_ref[...] = jnp.zeros_like(sb_ref)

    stages = []
    for j in range(n_chunks):
        stages.append(_ssd_local(j, fwd, j * SSD_CHUNK, False, msk_ref, scr))
        stages.append(_ssd_local(n_chunks + j, bwd, j * SSD_CHUNK, True, msk_ref, scr))
    _lockstep(stages)
    for j in range(n_chunks):
        _ssd_state_step(j, yf_ref, j * SSD_CHUNK, sf_ref, scr)
        jb = n_chunks - 1 - j
        _ssd_state_step(n_chunks + jb, yb_ref, jb * SSD_CHUNK, sb_ref, scr)


def _ssd_scan(xbc, dt, a_neg, *, tb=512):
    bsz, t, _ = xbc.shape
    g = MB_GROUPS
    nh = dt.shape[-1]
    e = nh // g
    w = nh * MB_HEAD
    gw = e * MB_HEAD
    tb = min(tb, t)
    nb = t // tb
    assert t % tb == 0 and tb % SSD_CHUNK == 0 and MB_STATE == LANES
    dt_g = dt.reshape(bsz, t, 2, g, e)
    adt_g = dt_g * a_neg.reshape(2, g, e)
    dt_tm = jnp.transpose(dt_g, (2, 0, 3, 1, 4))
    adt_tm = jnp.transpose(adt_g, (2, 0, 3, 1, 4))
    adt_hm = jnp.transpose(adt_g, (2, 0, 3, 4, 1))

    def specs(d, tmap):
        return [
            pl.BlockSpec((1, 1, 1, tb, e), lambda b, gi, i: (d, b, gi, tmap(i), 0)),
            pl.BlockSpec((1, 1, 1, tb, e), lambda b, gi, i: (d, b, gi, tmap(i), 0)),
            pl.BlockSpec((1, 1, 1, e, tb), lambda b, gi, i: (d, b, gi, 0, tmap(i))),
            pl.BlockSpec((1, tb, gw), lambda b, gi, i: (b, tmap(i), gi)),
            pl.BlockSpec((1, tb, MB_STATE), lambda b, gi, i: (b, tmap(i), w // MB_STATE + gi)),
            pl.BlockSpec((1, tb, MB_STATE), lambda b, gi, i: (b, tmap(i), w // MB_STATE + g + gi)),
        ]

    fmap = lambda i: i
    bmap = lambda i: nb - 1 - i
    out = jax.ShapeDtypeStruct((bsz, t, w), BF16)
    n_pairs = gw // LANES
    nch = tb // SSD_CHUNK
    msk = jnp.asarray(_ssd_masks())
    return pl.pallas_call(
        functools.partial(_ssd_kernel, n_chunks=nch),
        grid=(bsz, g, nb),
        in_specs=[pl.BlockSpec(msk.shape, lambda b, gi, i: (0, 0, 0))] + specs(0, fmap) + specs(1, bmap),
        out_specs=[pl.BlockSpec((1, tb, gw), lambda b, gi, i: (b, i, gi)),
                   pl.BlockSpec((1, tb, gw), lambda b, gi, i: (b, nb - 1 - i, gi))],
        out_shape=[out, out],
        scratch_shapes=[pltpu.VMEM((n_pairs, MB_STATE, LANES), F32),
                        pltpu.VMEM((n_pairs, MB_STATE, LANES), F32),
                        pltpu.VMEM((2 * nch, n_pairs, MB_STATE, LANES), F32),
                        pltpu.VMEM((2 * nch, n_pairs, SSD_CHUNK, LANES), F32),
                        pltpu.VMEM((2 * nch, n_pairs, 1, LANES), F32),
                        pltpu.VMEM((2 * nch, SSD_CHUNK, MB_STATE), BF16),
                        pltpu.VMEM((2 * nch, n_pairs, SSD_CHUNK, LANES), F32)],
        compiler_params=pltpu.CompilerParams(
            dimension_semantics=("parallel", "parallel", "arbitrary"),
            vmem_limit_bytes=VMEM_LIMIT),
        name="ssd_scan",
    )(msk, dt_tm, adt_tm, adt_hm, xbc, xbc, xbc, dt_tm, adt_tm, adt_hm, xbc, xbc, xbc)


def _layer_norm(x, g, b):
    mu = jnp.mean(x, -1, keepdims=True)
    xc = x - mu
    var = jnp.mean(xc * xc, -1, keepdims=True)
    return xc * lax.rsqrt(var + LN_EPS) * g + b


def _pad_cols(wm, n):
    return jnp.pad(wm, ((0, 0), (0, n - wm.shape[1])))


def _pad_rows(wm, n):
    return jnp.pad(wm, ((0, n - wm.shape[0]), (0, 0)))


def _stack_rows(vecs, rows=8):
    a = jnp.stack([v.reshape(-1).astype(F32) for v in vecs])
    return _pad_rows(a, -(-a.shape[0] // rows) * rows)


HALO = 16


def _halo_specs(tm, tn, n_rows, col_off=0):
    last = n_rows // HALO - 1
    rt = tm // HALO
    return [pl.BlockSpec((tm, tn), lambda i, j: (i, col_off + j)),
            pl.BlockSpec((HALO, tn), lambda i, j: (jnp.maximum(i * rt - 1, 0), col_off + j)),
            pl.BlockSpec((HALO, tn), lambda i, j: (jnp.minimum((i + 1) * rt, last), col_off + j))]


def _stage_rows(ext_ref, x_ref, prev_ref, next_ref, tiles_per_seq):
    tm = x_ref.shape[0]
    i = pl.program_id(0)
    first = (i % tiles_per_seq) == 0
    last = (i % tiles_per_seq) == tiles_per_seq - 1
    ext_ref[0:HALO, :] = jnp.where(first, 0.0, prev_ref[...].astype(F32))
    ext_ref[HALO:HALO + tm, :] = x_ref[...].astype(F32)
    ext_ref[HALO + tm:2 * HALO + tm, :] = jnp.where(last, 0.0, next_ref[...].astype(F32))


def _shifted(ext_ref, s, tm):
    return ext_ref[pl.ds(HALO + s, tm), :]


def _delta(ext_ref, tm):
    return 0.5 * (_shifted(ext_ref, -1, tm) + _shifted(ext_ref, 1, tm)) - _shifted(ext_ref, 0, tm)


def _softplus(x):
    return jnp.maximum(x, 0.0) + jnp.log(1.0 + jnp.exp(-jnp.abs(x)))


def _group_sum(x, e_ref):
    e = e_ref[...]
    outs = []
    for s in range(x.shape[1] // LANES):
        xs = x[:, s * LANES:(s + 1) * LANES]
        hi = xs.astype(BF16)
        lo = (xs - hi.astype(F32)).astype(BF16)
        outs.append(jnp.dot(hi, e, preferred_element_type=F32)
                    + jnp.dot(lo, e, preferred_element_type=F32))
    return outs[0] if len(outs) == 1 else jnp.concatenate(outs, axis=1)


def _head_indicator():
    i = np.arange(LANES) // RW_HEAD
    return jnp.asarray((i[:, None] == i[None, :]).astype(np.float32), dtype=BF16)


def _row_params(tm, seq_len):
    tm = min(tm, seq_len)
    assert seq_len % tm == 0 and tm % HALO == 0
    return tm, seq_len // tm


def _lora_in_kernel(h_ref, hp_ref, hn_ref, mu_ref, w_ref, o_ref, ext_ref, *, tiles_per_seq, n_lora):
    tm = h_ref.shape[0]
    _stage_rows(ext_ref, h_ref, hp_ref, hn_ref, tiles_per_seq)
    h = h_ref[...]
    hd = _delta(ext_ref, tm)
    src = (0, 0, 1, 1, 2)
    for i in range(n_lora):
        x = (h + hd * mu_ref[src[i]:src[i] + 1, :]).astype(BF16)
        mid = jnp.dot(x, w_ref[:, i * LANES:(i + 1) * LANES], preferred_element_type=F32)
        if i < 2:
            mid = jnp.tanh(mid)
        o_ref[:, i * LANES:(i + 1) * LANES] = mid.astype(o_ref.dtype)


def _lora_in(h, mus, w_first, seq_len, *, tm=512):
    m, d = h.shape
    tm, tps = _row_params(tm, seq_len)
    n_lora = len(w_first)
    w = jnp.concatenate([_pad_cols(x, LANES) for x in w_first], axis=1).astype(BF16)
    mu = _stack_rows(mus)
    return pl.pallas_call(
        functools.partial(_lora_in_kernel, tiles_per_seq=tps, n_lora=n_lora),
        grid=(m // tm, 1),
        in_specs=_halo_specs(tm, d, m) + [pl.BlockSpec(mu.shape, lambda i, j: (0, 0)),
                                          pl.BlockSpec(w.shape, lambda i, j: (0, 0))],
        out_specs=pl.BlockSpec((tm, n_lora * LANES), lambda i, j: (i, 0)),
        out_shape=jax.ShapeDtypeStruct((m, n_lora * LANES), BF16),
        scratch_shapes=[pltpu.VMEM((tm + 2 * HALO, d), F32)],
        compiler_params=pltpu.CompilerParams(
            dimension_semantics=("parallel", "arbitrary"), vmem_limit_bytes=VMEM_LIMIT),
        name="lora_in",
    )(h, h, h, mu, w)


_P_MU_R, _P_MU_K, _P_MU_V, _P_W0, _P_A0, _P_KK, _P_KA, _P_RK, _P_V0 = 0, 1, 2, 3, 5, 7, 8, 9, 10


def _rwkv_prep_kernel(*refs, tiles_per_seq, has_vres):
    (pr, prp, prn, pk, pkp, pkn, pv, pvp, pvn, mid_ref, w2_ref, par_ref, e_ref) = refs[:13]
    pos = 13
    vf_ref = None
    if has_vres:
        vf_ref = refs[pos]
        pos += 1
    r_o, k_o, v_o, kk_o, lw0_o, lw1_o, a0_o, a1_o, bonus_o = refs[pos:pos + 9]
    er, ek, ev = refs[pos + 9:pos + 12]
    tm = pr.shape[0]
    par = lambda i: par_ref[i:i + 1, :]

    outs = []
    for ext, x, xp, xn, mu in ((er, pr, prp, prn, _P_MU_R), (ek, pk, pkp, pkn, _P_MU_K),
                               (ev, pv, pvp, pvn, _P_MU_V)):
        _stage_rows(ext, x, xp, xn, tiles_per_seq)
        outs.append(x[...] + par(mu) * _delta(ext, tm))
    r, k, v = outs

    def lora(i):
        return jnp.dot(mid_ref[:, i * LANES:(i + 1) * LANES], w2_ref[i],
                       preferred_element_type=F32)

    kdir = 0.0
    for d, (lw_o, a_o) in enumerate(((lw0_o, a0_o), (lw1_o, a1_o))):
        w_log = -_softplus(-(par(_P_W0 + d) + lora(d))) - 0.5
        lw_o[...] = -jnp.exp(w_log)
        a = _sigmoid(par(_P_A0 + d) + lora(2 + d))
        a_o[...] = a.astype(a_o.dtype)
        kdir = kdir + (1.0 + (a - 1.0) * par(_P_KA))
    if has_vres:
        vg = _sigmoid(par(_P_V0) + lora(4))
        v = v + (vf_ref[...].astype(F32) - v) * vg
    kk = k * par(_P_KK)
    nrm = jnp.maximum(jnp.sqrt(_group_sum(kk * kk, e_ref)), 1e-12)
    r_o[...] = r.astype(r_o.dtype)
    k_o[...] = k.astype(k_o.dtype)
    v_o[...] = v.astype(v_o.dtype)
    kk_o[...] = (kk / nrm).astype(kk_o.dtype)
    bonus_o[...] = (_group_sum(r * k * kdir * par(_P_RK), e_ref) * v).astype(bonus_o.dtype)


def _rwkv_prep(p_rkv, mid, w_second, params, v_first, seq_len, *, tm=512, tn=256):
    m, w3 = p_rkv.shape
    w = w3 // 3
    tm, tps = _row_params(tm, seq_len)
    tn = min(tn, w)
    nj = w // tn
    assert w % tn == 0 and tn % LANES == 0
    has_vres = v_first is not None
    n_lora = len(w_second)
    w2 = jnp.stack([_pad_rows(x, LANES) for x in w_second]).astype(BF16)
    par = _stack_rows(params)
    tile = pl.BlockSpec((tm, tn), lambda i, j: (i, j))
    in_specs = (_halo_specs(tm, tn, m, 0) + _halo_specs(tm, tn, m, nj) + _halo_specs(tm, tn, m, 2 * nj)
                + [pl.BlockSpec((tm, mid.shape[1]), lambda i, j: (i, 0)),
                   pl.BlockSpec((n_lora, LANES, tn), lambda i, j: (0, 0, j)),
                   pl.BlockSpec((par.shape[0], tn), lambda i, j: (0, j)),
                   pl.BlockSpec((LANES, LANES), lambda i, j: (0, 0))])
    args = [p_rkv] * 9 + [mid, w2, par, _head_indicator()]
    if has_vres:
        in_specs.append(tile)
        args.append(v_first)
    half = jax.ShapeDtypeStruct((m, w), BF16)
    full = jax.ShapeDtypeStruct((m, w), F32)
    return pl.pallas_call(
        functools.partial(_rwkv_prep_kernel, tiles_per_seq=tps, has_vres=has_vres),
        grid=(m // tm, nj),
        in_specs=in_specs,
        out_specs=[tile] * 9,
        out_shape=[half] * 4 + [full] * 2 + [half] * 3,
        scratch_shapes=[pltpu.VMEM((tm + 2 * HALO, tn), F32)] * 3,
        compiler_params=pltpu.CompilerParams(
            dimension_semantics=("parallel", "parallel"), vmem_limit_bytes=VMEM_LIMIT),
        name="rwkv_prep",
    )(*args)


def _rwkv_post_kernel(yf, yb, bonus, gate, par_ref, e_ref, o_ref):
    lnx_g, lnx_b = par_ref[0:1, :], par_ref[1:2, :]
    inv_n = 1.0 / RW_HEAD
    y = yf[...].astype(F32) + yb[...].astype(F32)
    yc = y - _group_sum(y, e_ref) * inv_n
    var = _group_sum(yc * yc, e_ref) * inv_n
    gn = yc * lax.rsqrt(var + RW_GN_EPS) * lnx_g + lnx_b
    o_ref[...] = ((gn + bonus[...].astype(F32)) * gate[...].astype(F32)).astype(o_ref.dtype)


def _rwkv_post(y_f, y_b, bonus, gate, lnx_g, lnx_b, *, tm=512, tn=512):
    m, w = y_f.shape
    tm = min(tm, m)
    tn = min(tn, w)
    par = _stack_rows([lnx_g, lnx_b])
    tile = pl.BlockSpec((tm, tn), lambda i, j: (i, j))
    return pl.pallas_call(
        _rwkv_post_kernel,
        grid=(m // tm, w // tn),
        in_specs=[tile] * 4 + [pl.BlockSpec((par.shape[0], tn), lambda i, j: (0, j)),
                               pl.BlockSpec((LANES, LANES), lambda i, j: (0, 0))],
        out_specs=tile,
        out_shape=jax.ShapeDtypeStruct((m, w), BF16),
        compiler_params=pltpu.CompilerParams(
            dimension_semantics=("parallel", "parallel"), vmem_limit_bytes=VMEM_LIMIT),
        name="rwkv_post",
    )(y_f, y_b, bonus, gate, par, _head_indicator())


CONV_BLK = 128
CONV_PAD = 64


def _conv_shift_matrix():
    taps = [j for j in range(CONV_W) if j != CONV_W // 2]
    s = np.zeros((len(taps) * CONV_BLK, 2 * CONV_BLK), np.float32)
    for i, j in enumerate(taps):
        t = np.arange(CONV_BLK)
        s[i * CONV_BLK + t, CONV_PAD + t + j - CONV_W // 2] = 1.0
    return taps, s


def _conv_silu_kernel(x_ref, xp_ref, xn_ref, w_ref, s_ref, o_ref, ext_ref, *, tiles_per_seq):
    tm = x_ref.shape[0]
    i = pl.program_id(0)
    first = (i % tiles_per_seq) == 0
    last = (i % tiles_per_seq) == tiles_per_seq - 1
    zero = jnp.zeros((CONV_PAD - HALO, ext_ref.shape[1]), ext_ref.dtype)
    ext_ref[0:CONV_PAD - HALO, :] = zero
    halo0 = jnp.zeros(xp_ref.shape, xp_ref.dtype)
    ext_ref[CONV_PAD - HALO:CONV_PAD, :] = jnp.where(first, halo0, xp_ref[...])
    ext_ref[CONV_PAD:CONV_PAD + tm, :] = x_ref[...]
    ext_ref[CONV_PAD + tm:CONV_PAD + tm + HALO, :] = jnp.where(last, halo0, xn_ref[...])
    ext_ref[CONV_PAD + tm + HALO:2 * CONV_PAD + tm, :] = zero
    taps, _ = _conv_shift_matrix()
    mid = CONV_W // 2
    for b in range(tm // CONV_BLK):
        rows = slice(b * CONV_BLK, (b + 1) * CONV_BLK)
        src = ext_ref[b * CONV_BLK:(b + 2) * CONV_BLK, :]
        sh = jnp.dot(s_ref[...], src, preferred_element_type=F32)
        acc = w_ref[CONV_W:CONV_W + 1, :] + x_ref[rows, :].astype(F32) * w_ref[mid:mid + 1, :]
        for n, j in enumerate(taps):
            acc = acc + sh[n * CONV_BLK:(n + 1) * CONV_BLK] * w_ref[j:j + 1, :]
        o_ref[rows, :] = (acc * _sigmoid(acc)).astype(o_ref.dtype)


def _conv_silu(x, conv_w, conv_b, seq_len, *, tm=512, tn=512):
    m, ch = x.shape
    tm, tps = _row_params(tm, seq_len)
    tn = min(tn, ch)
    assert ch % tn == 0 and CONV_W // 2 <= HALO <= CONV_PAD and tm % CONV_BLK == 0
    wb = jnp.concatenate([conv_w, conv_b[None, :]], axis=0)
    shift = jnp.asarray(_conv_shift_matrix()[1], dtype=x.dtype)
    return pl.pallas_call(
        functools.partial(_conv_silu_kernel, tiles_per_seq=tps),
        grid=(m // tm, ch // tn),
        in_specs=_halo_specs(tm, tn, m) + [pl.BlockSpec((CONV_W + 1, tn), lambda i, j: (0, j)),
                                           pl.BlockSpec(shift.shape, lambda i, j: (0, 0))],
        out_specs=pl.BlockSpec((tm, tn), lambda i, j: (i, j)),
        out_shape=jax.ShapeDtypeStruct((m, ch), BF16),
        scratch_shapes=[pltpu.VMEM((tm + 2 * CONV_PAD, tn), x.dtype)],
        compiler_params=pltpu.CompilerParams(
            dimension_semantics=("parallel", "parallel"), vmem_limit_bytes=VMEM_LIMIT),
        name="conv_silu",
    )(x, x, x, wb, shift)


def _mamba_post_kernel(yf, yb, xm, gate, par_ref, o_ref):
    d_skip, norm_g = par_ref[0:1, :], par_ref[1:2, :]
    y = ((yf[...].astype(F32) + yb[...].astype(F32) + d_skip * xm[...].astype(F32))
         * gate[...].astype(F32))
    ms = jnp.mean(y * y, axis=-1, keepdims=True)
    o_ref[...] = (y * lax.rsqrt(ms + RMS_EPS) * norm_g).astype(o_ref.dtype)


def _mamba_post(ys_f, ys_b, xbc_c, z, d_skip_rep, norm_g, *, tm=512):
    m, w = z.shape
    tm = min(tm, m)
    gw = w // MB_GROUPS
    par = _stack_rows([d_skip_rep, norm_g])
    tile = pl.BlockSpec((tm, gw), lambda i, j: (i, j))
    return pl.pallas_call(
        _mamba_post_kernel,
        grid=(m // tm, MB_GROUPS),
        in_specs=[tile] * 4 + [pl.BlockSpec((par.shape[0], gw), lambda i, j: (0, j))],
        out_specs=tile,
        out_shape=jax.ShapeDtypeStruct((m, w), BF16),
        compiler_params=pltpu.CompilerParams(
            dimension_semantics=("parallel", "parallel"), vmem_limit_bytes=VMEM_LIMIT),
        name="mamba_post",
    )(ys_f, ys_b, xbc_c, z, par)


def _merge_kernel(x_ref, w_ref, pr_ref, ga_ref, gb_ref, o_ref):
    pm = jnp.dot(x_ref[...], w_ref[...], preferred_element_type=F32)
    o_ref[...] = (ga_ref[...].astype(F32) * pr_ref[...] + gb_ref[...].astype(F32) * pm).astype(o_ref.dtype)


def _merge_matmul(o_mb, p_mb, pr, gates, *, tm=512, tn=512):
    m, k = o_mb.shape
    n = p_mb.shape[1]
    tm = min(tm, m)
    tn = min(tn, n)
    nj = n // tn
    tile = pl.BlockSpec((tm, tn), lambda i, j: (i, j))
    return pl.pallas_call(
        _merge_kernel,
        grid=(m // tm, nj),
        in_specs=[pl.BlockSpec((tm, k), lambda i, j: (i, 0)),
                  pl.BlockSpec((k, tn), lambda i, j: (0, j)),
                  tile, tile, pl.BlockSpec((tm, tn), lambda i, j: (i, nj + j))],
        out_specs=tile,
        out_shape=jax.ShapeDtypeStruct((m, n), BF16),
        compiler_params=pltpu.CompilerParams(
            dimension_semantics=("parallel", "parallel"), vmem_limit_bytes=VMEM_LIMIT),
        name="merge_matmul",
    )(o_mb, p_mb, pr, gates, gates)


def _out_ln_kernel(x_ref, w_ref, h_ref, par_ref, o_ref, ob_ref):
    out = jnp.dot(x_ref[...], w_ref[...], preferred_element_type=F32)
    res = ALPHA * h_ref[...] + out
    mu = jnp.mean(res, axis=-1, keepdims=True)
    xc = res - mu
    var = jnp.mean(xc * xc, axis=-1, keepdims=True)
    hn = xc * lax.rsqrt(var + LN_EPS) * par_ref[0:1, :] + par_ref[1:2, :]
    o_ref[...] = hn
    ob_ref[...] = hn.astype(ob_ref.dtype)


def _out_ln(mix, w_out, h, ln_g, ln_b, *, tm=256, rows=None):
    k = mix.shape[1]
    n = w_out.shape[1]
    r0, r1 = rows or (0, mix.shape[0])
    m = r1 - r0
    tm = min(tm, m)
    assert r0 % tm == 0 and m % tm == 0
    b0 = r0 // tm
    par = _stack_rows([ln_g, ln_b])
    return pl.pallas_call(
        _out_ln_kernel,
        grid=(m // tm,),
        in_specs=[pl.BlockSpec((tm, k), lambda i: (i + b0, 0)),
                  pl.BlockSpec((k, n), lambda i: (0, 0)),
                  pl.BlockSpec((tm, n), lambda i: (i + b0, 0)),
                  pl.BlockSpec(par.shape, lambda i: (0, 0))],
        out_specs=[pl.BlockSpec((tm, n), lambda i: (i, 0))] * 2,
        out_shape=[jax.ShapeDtypeStruct((m, n), F32), jax.ShapeDtypeStruct((m, n), BF16)],
        compiler_params=pltpu.CompilerParams(
            dimension_semantics=("parallel",), vmem_limit_bytes=VMEM_LIMIT),
        name="out_ln",
    )(mix, w_out, h, par)


def _trunk(x, ln_in_g, ln_in_b, w_in, mu_rkv, mu_wa, w0, w1, w2, a0, a1, a2, mu_v, v0, v1, v2,
           k_k, k_a, r_k, lnx_g, lnx_b, p_rw, conv_w, conv_b, dt_bias, a_log, d_skip, mb_norm_g,
           p_mb, w_out, ln_g, ln_b, out_batches=None):
    bsz, t, d = x.shape
    m = bsz * t
    out_batches = out_batches or (bsz,)
    rw_w = d
    mb_w = 2 * d
    rw_heads = rw_w // RW_HEAD
    mb_heads = mb_w // MB_HEAD
    conv_ch = mb_w + 2 * MB_GROUPS * MB_STATE
    splits = [3 * rw_w, 4 * rw_w, 4 * rw_w + mb_w, 4 * rw_w + mb_w + conv_ch,
              4 * rw_w + mb_w + conv_ch + 2 * mb_heads]
    seq = lambda a: a.reshape(bsz, t, a.shape[-1])

    h = _layer_norm(x, ln_in_g, ln_in_b).reshape(m, d)
    hb = h.astype(BF16)
    v_first = None
    for l in range(DEPTH):
        wl = w_in[l].astype(BF16)
        bounds = [0] + splits + [wl.shape[1]]
        outs = ((F32, None), (BF16, "silu"), (BF16, "silu"), (BF16, None), (F32, None), (BF16, "sigmoid"))
        p_rkv, g_rw, z, xbc, dt_raw, gates = [
            _matmul(hb, wl[:, bounds[i]:bounds[i + 1]], out_dtype=outs[i][0], act=outs[i][1])
            for i in range(6)]

        mus = [mu_wa[l, 0], mu_wa[l, 1]]
        w_first = [w1[l, 0], w1[l, 1], a1[l, 0], a1[l, 1]]
        w_second = [w2[l, 0], w2[l, 1], a2[l, 0], a2[l, 1]]
        params = [mu_rkv[l, 0], mu_rkv[l, 1], mu_rkv[l, 2], w0[l, 0], w0[l, 1], a0[l, 0], a0[l, 1],
                  k_k[l], k_a[l], r_k[l]]
        if l > 0:
            mus.append(mu_v[l - 1])
            w_first.append(v1[l - 1])
            w_second.append(v2[l - 1])
            params.append(v0[l - 1])
        mid = _lora_in(h, mus, w_first, t)
        r, k, v, kk, lw0, lw1, a_0, a_1, bonus = _rwkv_prep(p_rkv, mid, w_second, params, v_first, t)
        if l == 0:
            v_first = v
        y_f, y_b = _rwkv_scan(seq(r), seq(k), seq(v), seq(kk), (seq(lw0), seq(lw1)),
                              (seq(a_0), seq(a_1)), k_a[l])
        o_rw = _rwkv_post(y_f.reshape(m, rw_w), y_b.reshape(m, rw_w), bonus, g_rw, lnx_g[l], lnx_b[l])

        xbc_c = _conv_silu(xbc, conv_w[l], conv_b[l], t)
        dt = jax.nn.softplus(dt_raw.reshape(bsz, t, 2, mb_heads) + dt_bias[l])
        a_neg = -jnp.exp(a_log[l])
        ys_f, ys_b = _ssd_scan(seq(xbc_c), dt, a_neg)
        o_mb = _mamba_post(ys_f.reshape(m, mb_w), ys_b.reshape(m, mb_w), xbc_c, z,
                           jnp.repeat(d_skip[l], MB_HEAD), mb_norm_g[l])

        pr = _matmul(o_rw, p_rw[l].astype(BF16))
        mix = _merge_matmul(o_mb, p_mb[l].astype(BF16), pr, gates)
        wo = w_out[l].astype(BF16)
        if l < DEPTH - 1:
            h, hb = _out_ln(mix, wo, h, ln_g[l], ln_b[l])
    bounds = [0]
    for nseq in out_batches:
        bounds.append(bounds[-1] + nseq * t)
    return [_out_ln(mix, wo, h, ln_g[-1], ln_b[-1], rows=(lo, hi))[0].reshape(-1, t, d)
            for lo, hi in zip(bounds[:-1], bounds[1:])]


def kernel(x_prompt, x_sample, ln_in_g, ln_in_b, w_in, mu_rkv, mu_wa, w0, w1, w2, a0, a1, a2, mu_v, v0, v1, v2, k_k, k_a, r_k, lnx_g, lnx_b, p_rw, conv_w, conv_b, dt_bias, a_log, d_skip, mb_norm_g, p_mb, w_out, ln_g, ln_b):
    assert x_prompt.shape[1:] == x_sample.shape[1:]
    nb = x_prompt.shape[0]
    x = jnp.concatenate([x_prompt, x_sample], axis=0)
    y_prompt, y_sample = _trunk(
        x, ln_in_g, ln_in_b, w_in, mu_rkv, mu_wa, w0, w1, w2, a0, a1, a2, mu_v, v0, v1, v2,
        k_k, k_a, r_k, lnx_g, lnx_b, p_rw, conv_w, conv_b, dt_bias, a_log, d_skip, mb_norm_g,
        p_mb, w_out, ln_g, ln_b, out_batches=(nb, x_sample.shape[0]))
    return (y_prompt.astype(x_prompt.dtype), y_sample.astype(x_sample.dtype))
```

```python
import functools
import math

import numpy as np
import jax
import jax.numpy as jnp
from jax import lax
from jax.experimental import pallas as pl
from jax.experimental.pallas import tpu as pltpu

F32 = jnp.float32
BF16 = jnp.bfloat16

DEPTH = 2
RW_HEAD = 64
RW_GN_EPS = 64e-5
MB_HEAD = 64
MB_GROUPS = 8
MB_STATE = 128
CONV_W = 7
ALPHA = (2 * DEPTH) ** 0.25
LN_EPS = 1e-5
RMS_EPS = 1e-5

LANES = 128
RW_CHUNK = 64
RW_GROUP = 8
SSD_CHUNK = 128
VMEM_LIMIT = 56 * 1024 * 1024


def _sigmoid(x):
    return 1.0 / (1.0 + jnp.exp(-x))


def _mm_kernel(x_ref, w_ref, o_ref, *, act):
    acc = jnp.dot(x_ref[...], w_ref[...], preferred_element_type=F32)
    if act == "sigmoid":
        acc = _sigmoid(acc)
    elif act == "silu":
        acc = acc * _sigmoid(acc)
    o_ref[...] = acc.astype(o_ref.dtype)


def _matmul(x, w, *, tm=1024, tn=1024, out_dtype=F32, act=None):
    m, k = x.shape
    n = w.shape[1]
    tm = min(tm, m)
    tn = min(tn, n)
    while n % tn:
        tn //= 2
    assert m % tm == 0 and (tn % LANES == 0 or tn == n)
    return pl.pallas_call(
        functools.partial(_mm_kernel, act=act),
        grid=(m // tm, n // tn),
        in_specs=[pl.BlockSpec((tm, k), lambda i, j: (i, 0)),
                  pl.BlockSpec((k, tn), lambda i, j: (0, j))],
        out_specs=pl.BlockSpec((tm, tn), lambda i, j: (i, j)),
        out_shape=jax.ShapeDtypeStruct((m, n), out_dtype),
        compiler_params=pltpu.CompilerParams(
            dimension_semantics=("parallel", "parallel"),
            vmem_limit_bytes=VMEM_LIMIT),
        name="dense_matmul",
    )(x, w)


def _dotf(a, b, exact=False):
    if exact:
        return jnp.dot(a, b, preferred_element_type=F32,
                       precision=lax.Precision.HIGHEST)
    return jnp.dot(a.astype(BF16), b.astype(BF16), preferred_element_type=F32)


def _dot_nt(a, b):
    return lax.dot_general(a.astype(BF16), b.astype(BF16),
                           (((1,), (1,)), ((), ())), preferred_element_type=F32)


def _dot_tn(a, b):
    return lax.dot_general(a.astype(BF16), b.astype(BF16),
                           (((0,), (0,)), ((), ())), preferred_element_type=F32)


def _dot_split(ones, x, terms):
    ob = ones.astype(BF16)
    acc = None
    for _ in range(terms):
        part = x.astype(BF16)
        x = x - part.astype(F32)
        term = jnp.dot(ob, part, preferred_element_type=F32)
        acc = term if acc is None else acc + term
    return acc


def _lockstep(stages):
    while stages:
        stages = [g for g in stages if next(g, StopIteration) is not StopIteration]


_M_SAME, _M_EYE, _M_STRICT, _M_INCL, _M_BASE, _M_LEVEL = 0, 1, 2, 3, 6, 7


def _rwkv_masks():
    c = RW_CHUNK
    i = np.arange(LANES)
    row, col = i[:, None], i[None, :]
    same = (row >= c) == (col >= c)
    tl, ts = row % c, col % c
    out = [same, row == col,
           same & (tl > ts), same & (tl >= ts),
           same & (tl < ts), same & (tl <= ts),
           same & ((tl // 2) == (ts // 2))]
    s = 2
    while s < c:
        out.append(same & ((tl // (2 * s)) == (ts // (2 * s))) & ((tl // s) != (ts // s)))
        s *= 2
    return np.stack(out).astype(np.float32)


def _rwkv_cum():
    i = np.arange(RW_CHUNK)
    return np.stack([i[None, :] <= i[:, None], i[None, :] >= i[:, None]]).astype(np.float32)


def _rwkv_local(ci, refs, row0, reverse, ka, msk_ref, cum_ref, scr):
    mw_s, n_s, q_s, yl_s, dec_s = scr
    r_ref, k_ref, v_ref, kk_ref, lw_ref, a_ref = refs
    c = RW_CHUNK
    d = 1 if reverse else 0
    sl = (0, pl.ds(row0, c), slice(None))
    r = r_ref[sl].astype(F32)
    k = k_ref[sl].astype(F32)
    v = v_ref[sl].astype(F32)
    kk = kk_ref[sl].astype(F32)
    lw = lw_ref[sl]
    a = a_ref[sl].astype(F32)
    same = msk_ref[_M_SAME]
    strict = msk_ref[_M_STRICT + 2 * d]
    incl = msk_ref[_M_INCL + 2 * d]

    cl = _dot_split(cum_ref[d], lw, 2)
    yield
    ce = cl - lw
    ct = cl[0:1, :] if reverse else cl[c - 1:c, :]
    kd = k * (1.0 + (a - 1.0) * ka)
    bv = kk * a
    einv = jnp.exp(-cl)
    dec_end = jnp.exp(ct - cl)
    dec_s[ci] = jnp.exp(ct)

    def bd(x):
        return jnp.concatenate([x, x], axis=0) * same

    xa = bd(-kk * jnp.exp(ce)).astype(BF16)
    xr = bd(r * jnp.exp(cl))
    vb = bd(v).astype(BF16)
    bh = bd(bv * dec_end).astype(BF16)
    kh = bd(kd * dec_end).astype(BF16)

    gram = _dot_nt(jnp.concatenate([xa, xr.astype(BF16)], axis=0),
                   jnp.concatenate([bd(bv * einv), bd(kd * einv)], axis=0))
    yield
    a_ab = gram[:LANES, :LANES] * strict
    a_ak = gram[:LANES, LANES:] * strict
    a_rb = (gram[LANES:, :LANES] * incl).astype(BF16)
    a_rk = (gram[LANES:, LANES:] * incl).astype(BF16)

    inv = msk_ref[_M_EYE] + a_ab * msk_ref[_M_BASE]
    g = _dotf(a_ak, vb)
    yk = _dotf(a_rk, vb)
    for lvl in range(int(math.log2(c)) - 1):
        s = 2 << lvl
        off = (a_ab * msk_ref[_M_LEVEL + lvl]).astype(BF16)
        ib = inv.astype(BF16)
        if s < 8:
            t = _dotf(ib, off)
            yield
            t = _dotf(t, ib)
            yield
            inv = inv + t
        else:
            pieces = [inv[i * s:(i + 1) * s] for i in range(LANES // s)]
            sel = [i for i in range(LANES // s) if (i % 2 == 0) == reverse]
            t = _dotf(jnp.concatenate([pieces[i] for i in sel], axis=0), off)
            yield
            t = _dotf(t, ib)
            yield
            for n, i in enumerate(sel):
                pieces[i] = pieces[i] + t[n * s:(n + 1) * s]
            inv = jnp.concatenate(pieces, axis=0)

    wu = _dotf(inv, jnp.concatenate([xa, g.astype(BF16)], axis=1))
    yield
    wu = wu.astype(BF16)
    mw = _dot_tn(wu[:, :LANES], bh)
    nn = _dot_tn(jnp.concatenate([wu[:, LANES:], vb], axis=0),
                 jnp.concatenate([bh, kh], axis=0))
    qy = _dotf(a_rb, wu)
    yield
    mw_s[ci] = mw.astype(BF16)
    n_s[ci] = nn
    q_s[ci] = (xr + qy[:, :LANES]).astype(BF16)
    yl = qy[:, LANES:] + yk
    yl_s[ci] = yl[:c] + yl[c:]


def _rwkv_state_steps(chunks, y_ref, h_ref, scr):
    mw_s, n_s, q_s, yl_s, dec_s = scr
    c = RW_CHUNK
    for ci, row0 in chunks:
        h = h_ref[...]
        hb = h.astype(BF16)
        y = _dot_nt(q_s[ci], hb)
        hm = _dotf(hb, mw_s[ci])
        yield
        y_ref[0, pl.ds(row0, c), :] = (y[:c] + y[c:] + yl_s[ci]).astype(y_ref.dtype)
        h_ref[...] = h * dec_s[ci] + hm + n_s[ci]


def _rwkv_kernel(ka_ref, msk_ref, cum_ref, *refs, n_chunks):
    fwd, bwd = refs[0:6], refs[6:12]
    yf_ref, yb_ref, hf_ref, hb_ref = refs[12:16]
    scr = refs[16:21]

    @pl.when(pl.program_id(2) == 0)
    def _():
        hf_ref[...] = jnp.zeros_like(hf_ref)
        hb_ref[...] = jnp.zeros_like(hb_ref)

    ka = ka_ref[...]
    order_f = list(range(n_chunks))
    order_b = list(range(n_chunks - 1, -1, -1))
    state = []
    for lo in range(0, n_chunks, RW_GROUP):
        hi = min(lo + RW_GROUP, n_chunks)
        local = []
        for jf, jb in zip(order_f[lo:hi], order_b[lo:hi]):
            local.append(_rwkv_local(jf, fwd, jf * RW_CHUNK, False, ka, msk_ref, cum_ref, scr))
            local.append(_rwkv_local(n_chunks + jb, bwd, jb * RW_CHUNK, True, ka, msk_ref, cum_ref, scr))
        _lockstep(local + state)
        state = [
            _rwkv_state_steps([(j, j * RW_CHUNK) for j in order_f[lo:hi]], yf_ref, hf_ref, scr),
            _rwkv_state_steps([(n_chunks + j, j * RW_CHUNK) for j in order_b[lo:hi]], yb_ref, hb_ref, scr)]
    _lockstep(state)


def _rwkv_scan(r, k, v, kk, lw, a, k_a, *, tb=2048):
    bsz, t, w = r.shape
    tb = min(tb, t)
    nb = t // tb
    nch = tb // RW_CHUNK
    assert t % tb == 0 and tb % RW_CHUNK == 0 and w % LANES == 0
    blk = (1, tb, LANES)
    f_map = lambda p, b, i: (b, i, p)
    b_map = lambda p, b, i: (b, nb - 1 - i, p)
    spec_f = pl.BlockSpec(blk, f_map)
    spec_b = pl.BlockSpec(blk, b_map)
    msk = jnp.asarray(_rwkv_masks())
    cum = jnp.asarray(_rwkv_cum())
    out = jax.ShapeDtypeStruct((bsz, t, w), BF16)
    return pl.pallas_call(
        functools.partial(_rwkv_kernel, n_chunks=nch),
        grid=(w // LANES, bsz, nb),
        in_specs=[pl.BlockSpec((1, LANES), lambda p, b, i: (0, p)),
                  pl.BlockSpec(msk.shape, lambda p, b, i: (0, 0, 0)),
                  pl.BlockSpec(cum.shape, lambda p, b, i: (0, 0, 0))]
                 + [spec_f] * 6 + [spec_b] * 6,
        out_specs=[spec_f, spec_b],
        out_shape=[out, out],
        scratch_shapes=[pltpu.VMEM((LANES, LANES), F32), pltpu.VMEM((LANES, LANES), F32),
                        pltpu.VMEM((2 * nch, LANES, LANES), BF16),
                        pltpu.VMEM((2 * nch, LANES, LANES), F32),
                        pltpu.VMEM((2 * nch, LANES, LANES), BF16),
                        pltpu.VMEM((2 * nch, RW_CHUNK, LANES), F32),
                        pltpu.VMEM((2 * nch, 1, LANES), F32)],
        compiler_params=pltpu.CompilerParams(
            dimension_semantics=("parallel", "parallel", "arbitrary"),
            vmem_limit_bytes=VMEM_LIMIT),
        name="rwkv7_scan",
    )(k_a.reshape(1, w), msk, cum, r, k, v, kk, lw[0], a[0], r, k, v, kk, lw[1], a[1])


def _ssd_masks():
    i = np.arange(SSD_CHUNK)
    fwd = i[None, :] <= i[:, None]
    bwd = i[None, :] >= i[:, None]
    return np.stack([fwd, fwd.T, bwd, bwd.T]).astype(np.float32)


def _ssd_local(ci, refs, row0, reverse, msk_ref, scr):
    new_s, ecol_s, etot_s, cc_s, yd_s = scr
    dtm_ref, adtm_ref, adth_ref, x_ref, b_ref, c_ref = refs
    c = SSD_CHUNK
    d = 2 if reverse else 0
    rows = pl.ds(row0, c)
    dt_tm = dtm_ref[0, 0, 0, rows, :]
    adt_tm = adtm_ref[0, 0, 0, rows, :]
    adt_hm = adth_ref[0, 0, 0, :, rows]
    bc = b_ref[0, rows, :].astype(BF16)
    cc = c_ref[0, rows, :].astype(BF16)
    cc_s[ci] = cc
    mask = msk_ref[d]

    acum_tm = _dotf(mask, adt_tm, exact=True)
    acum_hm = _dotf(adt_hm, msk_ref[d + 1], exact=True)
    cb = _dot_nt(cc, bc)
    yield
    tot = acum_hm[:, 0:1] if reverse else acum_hm[:, c - 1:c]
    bt = bc.T
    lane = lax.broadcasted_iota(jnp.int32, (c, LANES), 1)
    first = lane < MB_HEAD
    lane1 = lax.broadcasted_iota(jnp.int32, (1, LANES), 1) < MB_HEAD

    for q in range(x_ref.shape[2] // LANES):
        e1, e2 = 2 * q, 2 * q + 1
        cols = slice(q * LANES, (q + 1) * LANES)
        col1 = jnp.broadcast_to(acum_tm[:, e1:e1 + 1], (c, LANES))
        col2 = jnp.broadcast_to(acum_tm[:, e2:e2 + 1], (c, LANES))
        colp = jnp.where(first, col1, col2)
        dtp = jnp.where(first, jnp.broadcast_to(dt_tm[:, e1:e1 + 1], (c, LANES)),
                        jnp.broadcast_to(dt_tm[:, e2:e2 + 1], (c, LANES)))
        totp = jnp.where(lane1, jnp.broadcast_to(tot[e1:e1 + 1, :], (1, LANES)),
                         jnp.broadcast_to(tot[e2:e2 + 1, :], (1, LANES)))
        xp = x_ref[0, rows, cols] * dtp
        xpb = xp.astype(BF16)
        l1 = jnp.exp(jnp.minimum(col1 - acum_hm[e1:e1 + 1, :], 0.0)) * mask
        l2 = jnp.exp(jnp.minimum(col2 - acum_hm[e2:e2 + 1, :], 0.0)) * mask
        yd1 = _dotf(l1 * cb, xpb)
        yd2 = _dotf(l2 * cb, xpb)
        ns = _dotf(bt, jnp.exp(totp - colp) * xp)
        ecol_s[ci, q] = jnp.exp(colp)
        etot_s[ci, q] = jnp.exp(totp)
        yield
        yd_s[ci, q] = jnp.where(first, yd1, yd2)
        new_s[ci, q] = ns


def _ssd_state_step(ci, y_ref, row0, s_ref, scr):
    new_s, ecol_s, etot_s, cc_s, yd_s = scr
    rows = pl.ds(row0, SSD_CHUNK)
    for q in range(s_ref.shape[0]):
        cols = slice(q * LANES, (q + 1) * LANES)
        st = s_ref[q]
        y = yd_s[ci, q] + ecol_s[ci, q] * _dotf(cc_s[ci], st)
        y_ref[0, rows, cols] = y.astype(y_ref.dtype)
        s_ref[q] = etot_s[ci, q] * st + new_s[ci, q]


def _ssd_kernel(msk_ref, *refs, n_chunks):
    fwd, bwd = refs[0:6], refs[6:12]
    yf_ref, yb_ref, sf_ref, sb_ref = refs[12:16]
    scr = refs[16:21]

    @pl.when(pl.program_id(2) == 0)
    def _():
        sf_ref[...] = jnp.zeros_like(sf_ref)
        sb_ref[...] = jnp.zeros_like(sb_ref)

    stages = []
    for j in range(n_chunks):
        stages.append(_ssd_local(j, fwd, j * SSD_CHUNK, False, msk_ref, scr))
        stages.append(_ssd_local(n_chunks + j, bwd, j * SSD_CHUNK, True, msk_ref, scr))
    _lockstep(stages)
    for j in range(n_chunks):
        _ssd_state_step(j, yf_ref, j * SSD_CHUNK, sf_ref, scr)
        jb = n_chunks - 1 - j
        _ssd_state_step(n_chunks + jb, yb_ref, jb * SSD_CHUNK, sb_ref, scr)


def _ssd_scan(xbc, dt, a_neg, *, tb=512):
    bsz, t, _ = xbc.shape
    g = MB_GROUPS
    nh = dt.shape[-1]
    e = nh // g
    w = nh * MB_HEAD
    gw = e * MB_HEAD
    tb = min(tb, t)
    nb = t // tb
    assert t % tb == 0 and tb % SSD_CHUNK == 0 and MB_STATE == LANES
    dt_g = dt.reshape(bsz, t, 2, g, e)
    adt_g = dt_g * a_neg.reshape(2, g, e)
    dt_tm = jnp.transpose(dt_g, (2, 0, 3, 1, 4))
    adt_tm = jnp.transpose(adt_g, (2, 0, 3, 1, 4))
    adt_hm = jnp.transpose(adt_g, (2, 0, 3, 4, 1))

    def specs(d, tmap):
        return [
            pl.BlockSpec((1, 1, 1, tb, e), lambda b, gi, i: (d, b, gi, tmap(i), 0)),
            pl.BlockSpec((1, 1, 1, tb, e), lambda b, gi, i: (d, b, gi, tmap(i), 0)),
            pl.BlockSpec((1, 1, 1, e, tb), lambda b, gi, i: (d, b, gi, 0, tmap(i))),
            pl.BlockSpec((1, tb, gw), lambda b, gi, i: (b, tmap(i), gi)),
            pl.BlockSpec((1, tb, MB_STATE), lambda b, gi, i: (b, tmap(i), w // MB_STATE + gi)),
            pl.BlockSpec((1, tb, MB_STATE), lambda b, gi, i: (b, tmap(i), w // MB_STATE + g + gi)),
        ]

    fmap = lambda i: i
    bmap = lambda i: nb - 1 - i
    out = jax.ShapeDtypeStruct((bsz, t, w), BF16)
    n_pairs = gw // LANES
    nch = tb // SSD_CHUNK
    msk = jnp.asarray(_ssd_masks())
    return pl.pallas_call(
        functools.partial(_ssd_kernel, n_chunks=nch),
        grid=(bsz, g, nb),
        in_specs=[pl.BlockSpec(msk.shape, lambda b, gi, i: (0, 0, 0))] + specs(0, fmap) + specs(1, bmap),
        out_specs=[pl.BlockSpec((1, tb, gw), lambda b, gi, i: (b, i, gi)),
                   pl.BlockSpec((1, tb, gw), lambda b, gi, i: (b, nb - 1 - i, gi))],
        out_shape=[out, out],
        scratch_shapes=[pltpu.VMEM((n_pairs, MB_STATE, LANES), F32),
                        pltpu.VMEM((n_pairs, MB_STATE, LANES), F32),
                        pltpu.VMEM((2 * nch, n_pairs, MB_STATE, LANES), F32),
                        pltpu.VMEM((2 * nch, n_pairs, SSD_CHUNK, LANES), F32),
                        pltpu.VMEM((2 * nch, n_pairs, 1, LANES), F32),
                        pltpu.VMEM((2 * nch, SSD_CHUNK, MB_STATE), BF16),
                        pltpu.VMEM((2 * nch, n_pairs, SSD_CHUNK, LANES), F32)],
        compiler_params=pltpu.CompilerParams(
            dimension_semantics=("parallel", "parallel", "arbitrary"),
            vmem_limit_bytes=VMEM_LIMIT),
        name="ssd_scan",
    )(msk, dt_tm, adt_tm, adt_hm, xbc, xbc, xbc, dt_tm, adt_tm, adt_hm, xbc, xbc, xbc)


def _layer_norm(x, g, b):
    mu = jnp.mean(x, -1, keepdims=True)
    xc = x - mu
    var = jnp.mean(xc * xc, -1, keepdims=True)
    return xc * lax.rsqrt(var + LN_EPS) * g + b


def _pad_cols(wm, n):
    return jnp.pad(wm, ((0, 0), (0, n - wm.shape[1])))


def _pad_rows(wm, n):
    return jnp.pad(wm, ((0, n - wm.shape[0]), (0, 0)))


def _stack_rows(vecs, rows=8):
    a = jnp.stack([v.reshape(-1).astype(F32) for v in vecs])
    return _pad_rows(a, -(-a.shape[0] // rows) * rows)


HALO = 16


def _halo_specs(tm, tn, n_rows, col_off=0):
    last = n_rows // HALO - 1
    rt = tm // HALO
    return [pl.BlockSpec((tm, tn), lambda i, j: (i, col_off + j)),
            pl.BlockSpec((HALO, tn), lambda i, j: (jnp.maximum(i * rt - 1, 0), col_off + j)),
            pl.BlockSpec((HALO, tn), lambda i, j: (jnp.minimum((i + 1) * rt, last), col_off + j))]


def _stage_rows(ext_ref, x_ref, prev_ref, next_ref, tiles_per_seq):
    tm = x_ref.shape[0]
    i = pl.program_id(0)
    first = (i % tiles_per_seq) == 0
    last = (i % tiles_per_seq) == tiles_per_seq - 1
    ext_ref[0:HALO, :] = jnp.where(first, 0.0, prev_ref[...].astype(F32))
    ext_ref[HALO:HALO + tm, :] = x_ref[...].astype(F32)
    ext_ref[HALO + tm:2 * HALO + tm, :] = jnp.where(last, 0.0, next_ref[...].astype(F32))


def _shifted(ext_ref, s, tm):
    return ext_ref[pl.ds(HALO + s, tm), :]


def _delta(ext_ref, tm):
    return 0.5 * (_shifted(ext_ref, -1, tm) + _shifted(ext_ref, 1, tm)) - _shifted(ext_ref, 0, tm)


def _softplus(x):
    return jnp.maximum(x, 0.0) + jnp.log(1.0 + jnp.exp(-jnp.abs(x)))


def _group_sum(x, e_ref):
    e = e_ref[...]
    outs = []
    for s in range(x.shape[1] // LANES):
        xs = x[:, s * LANES:(s + 1) * LANES]
        hi = xs.astype(BF16)
        lo = (xs - hi.astype(F32)).astype(BF16)
        outs.append(jnp.dot(hi, e, preferred_element_type=F32)
                    + jnp.dot(lo, e, preferred_element_type=F32))
    return outs[0] if len(outs) == 1 else jnp.concatenate(outs, axis=1)


def _head_indicator():
    i = np.arange(LANES) // RW_HEAD
    return jnp.asarray((i[:, None] == i[None, :]).astype(np.float32), dtype=BF16)


def _row_params(tm, seq_len):
    tm = min(tm, seq_len)
    assert seq_len % tm == 0 and tm % HALO == 0
    return tm, seq_len // tm


def _lora_in_kernel(h_ref, hp_ref, hn_ref, mu_ref, w_ref, o_ref, ext_ref, *, tiles_per_seq, n_lora):
    tm = h_ref.shape[0]
    _stage_rows(ext_ref, h_ref, hp_ref, hn_ref, tiles_per_seq)
    h = h_ref[...]
    hd = _delta(ext_ref, tm)
    src = (0, 0, 1, 1, 2)
    for i in range(n_lora):
        x = (h + hd * mu_ref[src[i]:src[i] + 1, :]).astype(BF16)
        mid = jnp.dot(x, w_ref[:, i * LANES:(i + 1) * LANES], preferred_element_type=F32)
        if i < 2:
            mid = jnp.tanh(mid)
        o_ref[:, i * LANES:(i + 1) * LANES] = mid.astype(o_ref.dtype)


def _lora_in(h, mus, w_first, seq_len, *, tm=512):
    m, d = h.shape
    tm, tps = _row_params(tm, seq_len)
    n_lora = len(w_first)
    w = jnp.concatenate([_pad_cols(x, LANES) for x in w_first], axis=1).astype(BF16)
    mu = _stack_rows(mus)
    return pl.pallas_call(
        functools.partial(_lora_in_kernel, tiles_per_seq=tps, n_lora=n_lora),
        grid=(m // tm, 1),
        in_specs=_halo_specs(tm, d, m) + [pl.BlockSpec(mu.shape, lambda i, j: (0, 0)),
                                          pl.BlockSpec(w.shape, lambda i, j: (0, 0))],
        out_specs=pl.BlockSpec((tm, n_lora * LANES), lambda i, j: (i, 0)),
        out_shape=jax.ShapeDtypeStruct((m, n_lora * LANES), BF16),
        scratch_shapes=[pltpu.VMEM((tm + 2 * HALO, d), F32)],
        compiler_params=pltpu.CompilerParams(
            dimension_semantics=("parallel", "arbitrary"), vmem_limit_bytes=VMEM_LIMIT),
        name="lora_in",
    )(h, h, h, mu, w)


_P_MU_R, _P_MU_K, _P_MU_V, _P_W0, _P_A0, _P_KK, _P_KA, _P_RK, _P_V0 = 0, 1, 2, 3, 5, 7, 8, 9, 10


def _rwkv_prep_kernel(*refs, tiles_per_seq, has_vres):
    (pr, prp, prn, pk, pkp, pkn, pv, pvp, pvn, mid_ref, w2_ref, par_ref, e_ref) = refs[:13]
    pos = 13
    vf_ref = None
    if has_vres:
        vf_ref = refs[pos]
        pos += 1
    r_o, k_o, v_o, kk_o, lw0_o, lw1_o, a0_o, a1_o, bonus_o = refs[pos:pos + 9]
    er, ek, ev = refs[pos + 9:pos + 12]
    tm = pr.shape[0]
    par = lambda i: par_ref[i:i + 1, :]

    outs = []
    for ext, x, xp, xn, mu in ((er, pr, prp, prn, _P_MU_R), (ek, pk, pkp, pkn, _P_MU_K),
                               (ev, pv, pvp, pvn, _P_MU_V)):
        _stage_rows(ext, x, xp, xn, tiles_per_seq)
        outs.append(x[...] + par(mu) * _delta(ext, tm))
    r, k, v = outs

    def lora(i):
        return jnp.dot(mid_ref[:, i * LANES:(i + 1) * LANES], w2_ref[i],
                       preferred_element_type=F32)

    kdir = 0.0
    for d, (lw_o, a_o) in enumerate(((lw0_o, a0_o), (lw1_o, a1_o))):
        w_log = -_softplus(-(par(_P_W0 + d) + lora(d))) - 0.5
        lw_o[...] = -jnp.exp(w_log)
        a = _sigmoid(par(_P_A0 + d) + lora(2 + d))
        a_o[...] = a.astype(a_o.dtype)
        kdir = kdir + (1.0 + (a - 1.0) * par(_P_KA))
    if has_vres:
        vg = _sigmoid(par(_P_V0) + lora(4))
        v = v + (vf_ref[...].astype(F32) - v) * vg
    kk = k * par(_P_KK)
    nrm = jnp.maximum(jnp.sqrt(_group_sum(kk * kk, e_ref)), 1e-12)
    r_o[...] = r.astype(r_o.dtype)
    k_o[...] = k.astype(k_o.dtype)
    v_o[...] = v.astype(v_o.dtype)
    kk_o[...] = (kk / nrm).astype(kk_o.dtype)
    bonus_o[...] = (_group_sum(r * k * kdir * par(_P_RK), e_ref) * v).astype(bonus_o.dtype)


def _rwkv_prep(p_rkv, mid, w_second, params, v_first, seq_len, *, tm=512, tn=256):
    m, w3 = p_rkv.shape
    w = w3 // 3
    tm, tps = _row_params(tm, seq_len)
    tn = min(tn, w)
    nj = w // tn
    assert w % tn == 0 and tn % LANES == 0
    has_vres = v_first is not None
    n_lora = len(w_second)
    w2 = jnp.stack([_pad_rows(x, LANES) for x in w_second]).astype(BF16)
    par = _stack_rows(params)
    tile = pl.BlockSpec((tm, tn), lambda i, j: (i, j))
    in_specs = (_halo_specs(tm, tn, m, 0) + _halo_specs(tm, tn, m, nj) + _halo_specs(tm, tn, m, 2 * nj)
                + [pl.BlockSpec((tm, mid.shape[1]), lambda i, j: (i, 0)),
                   pl.BlockSpec((n_lora, LANES, tn), lambda i, j: (0, 0, j)),
                   pl.BlockSpec((par.shape[0], tn), lambda i, j: (0, j)),
                   pl.BlockSpec((LANES, LANES), lambda i, j: (0, 0))])
    args = [p_rkv] * 9 + [mid, w2, par, _head_indicator()]
    if has_vres:
        in_specs.append(tile)
        args.append(v_first)
    half = jax.ShapeDtypeStruct((m, w), BF16)
    full = jax.ShapeDtypeStruct((m, w), F32)
    return pl.pallas_call(
        functools.partial(_rwkv_prep_kernel, tiles_per_seq=tps, has_vres=has_vres),
        grid=(m // tm, nj),
        in_specs=in_specs,
        out_specs=[tile] * 9,
        out_shape=[half] * 4 + [full] * 2 + [half] * 3,
        scratch_shapes=[pltpu.VMEM((tm + 2 * HALO, tn), F32)] * 3,
        compiler_params=pltpu.CompilerParams(
            dimension_semantics=("parallel", "parallel"), vmem_limit_bytes=VMEM_LIMIT),
        name="rwkv_prep",
    )(*args)


def _branch_merge_kernel(yf_r, yb_r, bonus, g_r, par_r, e_ref, w_r,
                         yf_m, yb_m, xm, z, par_m, w_m, ga_ref, gb_ref,
                         o_ref, acc_r, acc_m, *, n_rw):
    k = pl.program_id(1)

    @pl.when(k == 0)
    def _():
        acc_r[...] = jnp.zeros_like(acc_r)
        acc_m[...] = jnp.zeros_like(acc_m)

    @pl.when(k < n_rw)
    def _():
        inv_n = 1.0 / RW_HEAD
        y = yf_r[...].astype(F32) + yb_r[...].astype(F32)
        yc = y - _group_sum(y, e_ref) * inv_n
        var = _group_sum(yc * yc, e_ref) * inv_n
        gn = yc * lax.rsqrt(var + RW_GN_EPS) * par_r[0:1, :] + par_r[1:2, :]
        x = ((gn + bonus[...].astype(F32)) * g_r[...].astype(F32)).astype(BF16)
        acc_r[...] += jnp.dot(x, w_r[...], preferred_element_type=F32)

    @pl.when(k >= n_rw)
    def _():
        y = ((yf_m[...].astype(F32) + yb_m[...].astype(F32) + par_m[0:1, :] * xm[...].astype(F32))
             * z[...].astype(F32))
        ms = jnp.mean(y * y, axis=-1, keepdims=True)
        x = (y * lax.rsqrt(ms + RMS_EPS) * par_m[1:2, :]).astype(BF16)
        acc_m[...] += jnp.dot(x, w_m[...], preferred_element_type=F32)

    @pl.when(k == pl.num_programs(1) - 1)
    def _():
        o_ref[...] = (ga_ref[...].astype(F32) * acc_r[...]
                      + gb_ref[...].astype(F32) * acc_m[...]).astype(o_ref.dtype)


def _branch_merge(y_f, y_b, bonus, g_rw, lnx_g, lnx_b, p_rw,
                  ys_f, ys_b, xbc_c, z, d_skip_rep, norm_g, p_mb, gates, *, tm=512, kw=512):
    m, rw_w = y_f.shape
    mb_w = z.shape[1]
    d = p_rw.shape[1]
    tm = min(tm, m)
    kr = min(kw, rw_w)
    km = mb_w // MB_GROUPS
    n_rw, n_mb = rw_w // kr, MB_GROUPS
    assert rw_w % kr == 0 and kr % LANES == 0 and gates.shape[1] == 2 * d
    par_r = _stack_rows([lnx_g, lnx_b])
    par_m = _stack_rows([d_skip_rep, norm_g])
    rcol = lambda i, k: (i, jnp.minimum(k, n_rw - 1))
    mcol = lambda i, k: (i, jnp.maximum(k - n_rw, 0))
    rw_tile = pl.BlockSpec((tm, kr), rcol)
    mb_tile = pl.BlockSpec((tm, km), mcol)
    return pl.pallas_call(
        functools.partial(_branch_merge_kernel, n_rw=n_rw),
        grid=(m // tm, n_rw + n_mb),
        in_specs=[rw_tile] * 4
                 + [pl.BlockSpec((par_r.shape[0], kr), lambda i, k: (0, jnp.minimum(k, n_rw - 1))),
                    pl.BlockSpec((LANES, LANES), lambda i, k: (0, 0)),
                    pl.BlockSpec((kr, d), lambda i, k: (jnp.minimum(k, n_rw - 1), 0))]
                 + [mb_tile] * 4
                 + [pl.BlockSpec((par_m.shape[0], km), lambda i, k: (0, jnp.maximum(k - n_rw, 0))),
                    pl.BlockSpec((km, d), lambda i, k: (jnp.maximum(k - n_rw, 0), 0)),
                    pl.BlockSpec((tm, d), lambda i, k: (i, 0)),
                    pl.BlockSpec((tm, d), lambda i, k: (i, 1))],
        out_specs=pl.BlockSpec((tm, d), lambda i, k: (i, 0)),
        out_shape=jax.ShapeDtypeStruct((m, d), BF16),
        scratch_shapes=[pltpu.VMEM((tm, d), F32), pltpu.VMEM((tm, d), F32)],
        compiler_params=pltpu.CompilerParams(
            dimension_semantics=("parallel", "arbitrary"), vmem_limit_bytes=VMEM_LIMIT),
        name="branch_merge",
    )(y_f, y_b, bonus, g_rw, par_r, _head_indicator(), p_rw,
      ys_f, ys_b, xbc_c, z, par_m, p_mb, gates, gates)


CONV_BLK = 128
CONV_PAD = 64


def _conv_shift_matrix():
    taps = [j for j in range(CONV_W) if j != CONV_W // 2]
    s = np.zeros((len(taps) * CONV_BLK, 2 * CONV_BLK), np.float32)
    for i, j in enumerate(taps):
        t = np.arange(CONV_BLK)
        s[i * CONV_BLK + t, CONV_PAD + t + j - CONV_W // 2] = 1.0
    return taps, s


def _conv_silu_kernel(x_ref, xp_ref, xn_ref, w_ref, s_ref, o_ref, ext_ref, *, tiles_per_seq):
    tm = x_ref.shape[0]
    i = pl.program_id(0)
    first = (i % tiles_per_seq) == 0
    last = (i % tiles_per_seq) == tiles_per_seq - 1
    zero = jnp.zeros((CONV_PAD - HALO, ext_ref.shape[1]), ext_ref.dtype)
    ext_ref[0:CONV_PAD - HALO, :] = zero
    halo0 = jnp.zeros(xp_ref.shape, xp_ref.dtype)
    ext_ref[CONV_PAD - HALO:CONV_PAD, :] = jnp.where(first, halo0, xp_ref[...])
    ext_ref[CONV_PAD:CONV_PAD + tm, :] = x_ref[...]
    ext_ref[CONV_PAD + tm:CONV_PAD + tm + HALO, :] = jnp.where(last, halo0, xn_ref[...])
    ext_ref[CONV_PAD + tm + HALO:2 * CONV_PAD + tm, :] = zero
    taps, _ = _conv_shift_matrix()
    mid = CONV_W // 2
    for b in range(tm // CONV_BLK):
        rows = slice(b * CONV_BLK, (b + 1) * CONV_BLK)
        src = ext_ref[b * CONV_BLK:(b + 2) * CONV_BLK, :]
        sh = jnp.dot(s_ref[...], src, preferred_element_type=F32)
        acc = w_ref[CONV_W:CONV_W + 1, :] + x_ref[rows, :].astype(F32) * w_ref[mid:mid + 1, :]
        for n, j in enumerate(taps):
            acc = acc + sh[n * CONV_BLK:(n + 1) * CONV_BLK] * w_ref[j:j + 1, :]
        o_ref[rows, :] = (acc * _sigmoid(acc)).astype(o_ref.dtype)


def _conv_silu(x, conv_w, conv_b, seq_len, *, tm=512, tn=512):
    m, ch = x.shape
    tm, tps = _row_params(tm, seq_len)
    tn = min(tn, ch)
    assert ch % tn == 0 and CONV_W // 2 <= HALO <= CONV_PAD and tm % CONV_BLK == 0
    wb = jnp.concatenate([conv_w, conv_b[None, :]], axis=0)
    shift = jnp.asarray(_conv_shift_matrix()[1], dtype=x.dtype)
    return pl.pallas_call(
        functools.partial(_conv_silu_kernel, tiles_per_seq=tps),
        grid=(m // tm, ch // tn),
        in_specs=_halo_specs(tm, tn, m) + [pl.BlockSpec((CONV_W + 1, tn), lambda i, j: (0, j)),
                                           pl.BlockSpec(shift.shape, lambda i, j: (0, 0))],
        out_specs=pl.BlockSpec((tm, tn), lambda i, j: (i, j)),
        out_shape=jax.ShapeDtypeStruct((m, ch), BF16),
        scratch_shapes=[pltpu.VMEM((tm + 2 * CONV_PAD, tn), x.dtype)],
        compiler_params=pltpu.CompilerParams(
            dimension_semantics=("parallel", "parallel"), vmem_limit_bytes=VMEM_LIMIT),
        name="conv_silu",
    )(x, x, x, wb, shift)


def _out_ln_kernel(x_ref, w_ref, h_ref, par_ref, o_ref, ob_ref):
    out = jnp.dot(x_ref[...], w_ref[...], preferred_element_type=F32)
    res = ALPHA * h_ref[...] + out
    mu = jnp.mean(res, axis=-1, keepdims=True)
    xc = res - mu
    var = jnp.mean(xc * xc, axis=-1, keepdims=True)
    hn = xc * lax.rsqrt(var + LN_EPS) * par_ref[0:1, :] + par_ref[1:2, :]
    o_ref[...] = hn
    ob_ref[...] = hn.astype(ob_ref.dtype)


def _out_ln(mix, w_out, h, ln_g, ln_b, *, tm=256, rows=None):
    k = mix.shape[1]
    n = w_out.shape[1]
    r0, r1 = rows or (0, mix.shape[0])
    m = r1 - r0
    tm = min(tm, m)
    assert r0 % tm == 0 and m % tm == 0
    b0 = r0 // tm
    par = _stack_rows([ln_g, ln_b])
    return pl.pallas_call(
        _out_ln_kernel,
        grid=(m // tm,),
        in_specs=[pl.BlockSpec((tm, k), lambda i: (i + b0, 0)),
                  pl.BlockSpec((k, n), lambda i: (0, 0)),
                  pl.BlockSpec((tm, n), lambda i: (i + b0, 0)),
                  pl.BlockSpec(par.shape, lambda i: (0, 0))],
        out_specs=[pl.BlockSpec((tm, n), lambda i: (i, 0))] * 2,
        out_shape=[jax.ShapeDtypeStruct((m, n), F32), jax.ShapeDtypeStruct((m, n), BF16)],
        compiler_params=pltpu.CompilerParams(
            dimension_semantics=("parallel",), vmem_limit_bytes=VMEM_LIMIT),
        name="out_ln",
    )(mix, w_out, h, par)


def _trunk(x, ln_in_g, ln_in_b, w_in, mu_rkv, mu_wa, w0, w1, w2, a0, a1, a2, mu_v, v0, v1, v2,
           k_k, k_a, r_k, lnx_g, lnx_b, p_rw, conv_w, conv_b, dt_bias, a_log, d_skip, mb_norm_g,
           p_mb, w_out, ln_g, ln_b, out_batches=None):
    bsz, t, d = x.shape
    m = bsz * t
    out_batches = out_batches or (bsz,)
    rw_w = d
    mb_w = 2 * d
    rw_heads = rw_w // RW_HEAD
    mb_heads = mb_w // MB_HEAD
    conv_ch = mb_w + 2 * MB_GROUPS * MB_STATE
    splits = [3 * rw_w, 4 * rw_w, 4 * rw_w + mb_w, 4 * rw_w + mb_w + conv_ch,
              4 * rw_w + mb_w + conv_ch + 2 * mb_heads]
    seq = lambda a: a.reshape(bsz, t, a.shape[-1])

    h = _layer_norm(x, ln_in_g, ln_in_b).reshape(m, d)
    hb = h.astype(BF16)
    v_first = None
    for l in range(DEPTH):
        wl = w_in[l].astype(BF16)
        bounds = [0] + splits + [wl.shape[1]]
        outs = ((F32, None), (BF16, "silu"), (BF16, "silu"), (BF16, None), (F32, None), (BF16, "sigmoid"))
        p_rkv, g_rw, z, xbc, dt_raw, gates = [
            _matmul(hb, wl[:, bounds[i]:bounds[i + 1]], out_dtype=outs[i][0], act=outs[i][1])
            for i in range(6)]

        mus = [mu_wa[l, 0], mu_wa[l, 1]]
        w_first = [w1[l, 0], w1[l, 1], a1[l, 0], a1[l, 1]]
        w_second = [w2[l, 0], w2[l, 1], a2[l, 0], a2[l, 1]]
        params = [mu_rkv[l, 0], mu_rkv[l, 1], mu_rkv[l, 2], w0[l, 0], w0[l, 1], a0[l, 0], a0[l, 1],
                  k_k[l], k_a[l], r_k[l]]
        if l > 0:
            mus.append(mu_v[l - 1])
            w_first.append(v1[l - 1])
            w_second.append(v2[l - 1])
            params.append(v0[l - 1])
        mid = _lora_in(h, mus, w_first, t)
        r, k, v, kk, lw0, lw1, a_0, a_1, bonus = _rwkv_prep(p_rkv, mid, w_second, params, v_first, t)
        if l == 0:
            v_first = v
        y_f, y_b = _rwkv_scan(seq(r), seq(k), seq(v), seq(kk), (seq(lw0), seq(lw1)),
                              (seq(a_0), seq(a_1)), k_a[l])

        xbc_c = _conv_silu(xbc, conv_w[l], conv_b[l], t)
        dt = jax.nn.softplus(dt_raw.reshape(bsz, t, 2, mb_heads) + dt_bias[l])
        a_neg = -jnp.exp(a_log[l])
        ys_f, ys_b = _ssd_scan(seq(xbc_c), dt, a_neg)

        mix = _branch_merge(y_f.reshape(m, rw_w), y_b.reshape(m, rw_w), bonus, g_rw,
                            lnx_g[l], lnx_b[l], p_rw[l].astype(BF16),
                            ys_f.reshape(m, mb_w), ys_b.reshape(m, mb_w), xbc_c, z,
                            jnp.repeat(d_skip[l], MB_HEAD), mb_norm_g[l], p_mb[l].astype(BF16), gates)
        wo = w_out[l].astype(BF16)
        if l < DEPTH - 1:
            h, hb = _out_ln(mix, wo, h, ln_g[l], ln_b[l])
    bounds = [0]
    for nseq in out_batches:
        bounds.append(bounds[-1] + nseq * t)
    return [_out_ln(mix, wo, h, ln_g[-1], ln_b[-1], rows=(lo, hi))[0].reshape(-1, t, d)
            for lo, hi in zip(bounds[:-1], bounds[1:])]


def kernel(x_prompt, x_sample, ln_in_g, ln_in_b, w_in, mu_rkv, mu_wa, w0, w1, w2, a0, a1, a2, mu_v, v0, v1, v2, k_k, k_a, r_k, lnx_g, lnx_b, p_rw, conv_w, conv_b, dt_bias, a_log, d_skip, mb_norm_g, p_mb, w_out, ln_g, ln_b):
    assert x_prompt.shape[1:] == x_sample.shape[1:]
    nb = x_prompt.shape[0]
    x = jnp.concatenate([x_prompt, x_sample], axis=0)
    y_prompt, y_sample = _trunk(
        x, ln_in_g, ln_in_b, w_in, mu_rkv, mu_wa, w0, w1, w2, a0, a1, a2, mu_v, v0, v1, v2,
        k_k, k_a, r_k, lnx_g, lnx_b, p_rw, conv_w, conv_b, dt_bias, a_log, d_skip, mb_norm_g,
        p_mb, w_out, ln_g, ln_b, out_batches=(nb, x_sample.shape[0]))
    return (y_prompt.astype(x_prompt.dtype), y_sample.astype(x_sample.dtype))
```

```python
import functools
import math

import numpy as np
import jax
import jax.numpy as jnp
from jax import lax
from jax.experimental import pallas as pl
from jax.experimental.pallas import tpu as pltpu

F32 = jnp.float32
BF16 = jnp.bfloat16

DEPTH = 2
RW_HEAD = 64
RW_GN_EPS = 64e-5
MB_HEAD = 64
MB_GROUPS = 8
MB_STATE = 128
CONV_W = 7
ALPHA = (2 * DEPTH) ** 0.25
LN_EPS = 1e-5
RMS_EPS = 1e-5

LANES = 128
RW_CHUNK = 64
RW_GROUP = 8
SSD_CHUNK = 128
VMEM_LIMIT = 56 * 1024 * 1024


def _sigmoid(x):
    return 1.0 / (1.0 + jnp.exp(-x))


def _mm_kernel(x_ref, w_ref, o_ref, *, act):
    acc = jnp.dot(x_ref[...], w_ref[...], preferred_element_type=F32)
    if act == "sigmoid":
        acc = _sigmoid(acc)
    elif act == "silu":
        acc = acc * _sigmoid(acc)
    o_ref[...] = acc.astype(o_ref.dtype)


def _matmul(x, w, *, tm=1024, tn=1024, out_dtype=F32, act=None):
    m, k = x.shape
    n = w.shape[1]
    tm = min(tm, m)
    tn = min(tn, n)
    while n % tn:
        tn //= 2
    assert m % tm == 0 and (tn % LANES == 0 or tn == n)
    return pl.pallas_call(
        functools.partial(_mm_kernel, act=act),
        grid=(m // tm, n // tn),
        in_specs=[pl.BlockSpec((tm, k), lambda i, j: (i, 0)),
                  pl.BlockSpec((k, tn), lambda i, j: (0, j))],
        out_specs=pl.BlockSpec((tm, tn), lambda i, j: (i, j)),
        out_shape=jax.ShapeDtypeStruct((m, n), out_dtype),
        compiler_params=pltpu.CompilerParams(
            dimension_semantics=("parallel", "parallel"),
            vmem_limit_bytes=VMEM_LIMIT),
        name="dense_matmul",
    )(x, w)


def _dotf(a, b, exact=False):
    if exact:
        return jnp.dot(a, b, preferred_element_type=F32,
                       precision=lax.Precision.HIGHEST)
    return jnp.dot(a.astype(BF16), b.astype(BF16), preferred_element_type=F32)


def _dot_nt(a, b):
    return lax.dot_general(a.astype(BF16), b.astype(BF16),
                           (((1,), (1,)), ((), ())), preferred_element_type=F32)


def _dot_tn(a, b):
    return lax.dot_general(a.astype(BF16), b.astype(BF16),
                           (((0,), (0,)), ((), ())), preferred_element_type=F32)


def _dot_split(ones, x, terms):
    ob = ones.astype(BF16)
    acc = None
    for _ in range(terms):
        part = x.astype(BF16)
        x = x - part.astype(F32)
        term = jnp.dot(ob, part, preferred_element_type=F32)
        acc = term if acc is None else acc + term
    return acc


def _lockstep(stages):
    while stages:
        stages = [g for g in stages if next(g, StopIteration) is not StopIteration]


_M_SAME, _M_EYE, _M_STRICT, _M_INCL, _M_BASE, _M_LEVEL = 0, 1, 2, 3, 6, 7


def _rwkv_masks():
    c = RW_CHUNK
    i = np.arange(LANES)
    row, col = i[:, None], i[None, :]
    same = (row >= c) == (col >= c)
    tl, ts = row % c, col % c
    out = [same, row == col,
           same & (tl > ts), same & (tl >= ts),
           same & (tl < ts), same & (tl <= ts),
           same & ((tl // 2) == (ts // 2))]
    s = 2
    while s < c:
        out.append(same & ((tl // (2 * s)) == (ts // (2 * s))) & ((tl // s) != (ts // s)))
        s *= 2
    return np.stack(out).astype(np.float32)


def _rwkv_cum():
    i = np.arange(RW_CHUNK)
    return np.stack([i[None, :] <= i[:, None], i[None, :] >= i[:, None]]).astype(np.float32)


def _rwkv_local(ci, refs, row0, reverse, ka, msk_ref, cum_ref, scr):
    mw_s, n_s, q_s, yl_s, dec_s = scr
    r_ref, k_ref, v_ref, kk_ref, lw_ref, a_ref = refs
    c = RW_CHUNK
    d = 1 if reverse else 0
    sl = (0, pl.ds(row0, c), slice(None))
    r = r_ref[sl].astype(F32)
    k = k_ref[sl].astype(F32)
    v = v_ref[sl].astype(F32)
    kk = kk_ref[sl].astype(F32)
    lw = lw_ref[sl]
    a = a_ref[sl].astype(F32)
    same = msk_ref[_M_SAME]
    strict = msk_ref[_M_STRICT + 2 * d]
    incl = msk_ref[_M_INCL + 2 * d]

    cl = _dot_split(cum_ref[d], lw, 2)
    yield
    ce = cl - lw
    ct = cl[0:1, :] if reverse else cl[c - 1:c, :]
    kd = k * (1.0 + (a - 1.0) * ka)
    bv = kk * a
    einv = jnp.exp(-cl)
    dec_end = jnp.exp(ct - cl)
    dec_s[ci] = jnp.exp(ct)

    def bd(x):
        return jnp.concatenate([x, x], axis=0) * same

    xa = bd(-kk * jnp.exp(ce)).astype(BF16)
    xr = bd(r * jnp.exp(cl))
    vb = bd(v).astype(BF16)
    bh = bd(bv * dec_end).astype(BF16)
    kh = bd(kd * dec_end).astype(BF16)

    gram = _dot_nt(jnp.concatenate([xa, xr.astype(BF16)], axis=0),
                   jnp.concatenate([bd(bv * einv), bd(kd * einv)], axis=0))
    yield
    a_ab = gram[:LANES, :LANES] * strict
    a_ak = gram[:LANES, LANES:] * strict
    a_rb = (gram[LANES:, :LANES] * incl).astype(BF16)
    a_rk = (gram[LANES:, LANES:] * incl).astype(BF16)

    inv = msk_ref[_M_EYE] + a_ab * msk_ref[_M_BASE]
    g = _dotf(a_ak, vb)
    yk = _dotf(a_rk, vb)
    for lvl in range(int(math.log2(c)) - 1):
        s = 2 << lvl
        off = (a_ab * msk_ref[_M_LEVEL + lvl]).astype(BF16)
        ib = inv.astype(BF16)
        if s < 8:
            t = _dotf(ib, off)
            yield
            t = _dotf(t, ib)
            yield
            inv = inv + t
        else:
            pieces = [inv[i * s:(i + 1) * s] for i in range(LANES // s)]
            sel = [i for i in range(LANES // s) if (i % 2 == 0) == reverse]
            t = _dotf(jnp.concatenate([pieces[i] for i in sel], axis=0), off)
            yield
            t = _dotf(t, ib)
            yield
            for n, i in enumerate(sel):
                pieces[i] = pieces[i] + t[n * s:(n + 1) * s]
            inv = jnp.concatenate(pieces, axis=0)

    wu = _dotf(inv, jnp.concatenate([xa, g.astype(BF16)], axis=1))
    yield
    wu = wu.astype(BF16)
    mw = _dot_tn(wu[:, :LANES], bh)
    nn = _dot_tn(jnp.concatenate([wu[:, LANES:], vb], axis=0),
                 jnp.concatenate([bh, kh], axis=0))
    qy = _dotf(a_rb, wu)
    yield
    mw_s[ci] = mw.astype(BF16)
    n_s[ci] = nn
    q_s[ci] = (xr + qy[:, :LANES]).astype(BF16)
    yl = qy[:, LANES:] + yk
    yl_s[ci] = yl[:c] + yl[c:]


def _rwkv_state_steps(chunks, y_ref, h_ref, scr):
    mw_s, n_s, q_s, yl_s, dec_s = scr
    c = RW_CHUNK
    for ci, row0 in chunks:
        h = h_ref[...]
        hb = h.astype(BF16)
        y = _dot_nt(q_s[ci], hb)
        hm = _dotf(hb, mw_s[ci])
        yield
        y_ref[0, pl.ds(row0, c), :] = (y[:c] + y[c:] + yl_s[ci]).astype(y_ref.dtype)
        h_ref[...] = h * dec_s[ci] + hm + n_s[ci]


def _rwkv_kernel(ka_ref, msk_ref, cum_ref, *refs, n_chunks):
    fwd, bwd = refs[0:6], refs[6:12]
    yf_ref, yb_ref, hf_ref, hb_ref = refs[12:16]
    scr = refs[16:21]

    @pl.when(pl.program_id(2) == 0)
    def _():
        hf_ref[...] = jnp.zeros_like(hf_ref)
        hb_ref[...] = jnp.zeros_like(hb_ref)

    ka = ka_ref[...]
    order_f = list(range(n_chunks))
    order_b = list(range(n_chunks - 1, -1, -1))
    state = []
    for lo in range(0, n_chunks, RW_GROUP):
        hi = min(lo + RW_GROUP, n_chunks)
        local = []
        for jf, jb in zip(order_f[lo:hi], order_b[lo:hi]):
            local.append(_rwkv_local(jf, fwd, jf * RW_CHUNK, False, ka, msk_ref, cum_ref, scr))
            local.append(_rwkv_local(n_chunks + jb, bwd, jb * RW_CHUNK, True, ka, msk_ref, cum_ref, scr))
        _lockstep(local + state)
        state = [
            _rwkv_state_steps([(j, j * RW_CHUNK) for j in order_f[lo:hi]], yf_ref, hf_ref, scr),
            _rwkv_state_steps([(n_chunks + j, j * RW_CHUNK) for j in order_b[lo:hi]], yb_ref, hb_ref, scr)]
    _lockstep(state)


def _rwkv_scan(r, k, v, kk, lw, a, k_a, *, tb=2048):
    bsz, t, w = r.shape
    tb = min(tb, t)
    nb = t // tb
    nch = tb // RW_CHUNK
    assert t % tb == 0 and tb % RW_CHUNK == 0 and w % LANES == 0
    blk = (1, tb, LANES)
    f_map = lambda p, b, i: (b, i, p)
    b_map = lambda p, b, i: (b, nb - 1 - i, p)
    spec_f = pl.BlockSpec(blk, f_map)
    spec_b = pl.BlockSpec(blk, b_map)
    msk = jnp.asarray(_rwkv_masks())
    cum = jnp.asarray(_rwkv_cum())
    out = jax.ShapeDtypeStruct((bsz, t, w), BF16)
    return pl.pallas_call(
        functools.partial(_rwkv_kernel, n_chunks=nch),
        grid=(w // LANES, bsz, nb),
        in_specs=[pl.BlockSpec((1, LANES), lambda p, b, i: (0, p)),
                  pl.BlockSpec(msk.shape, lambda p, b, i: (0, 0, 0)),
                  pl.BlockSpec(cum.shape, lambda p, b, i: (0, 0, 0))]
                 + [spec_f] * 6 + [spec_b] * 6,
        out_specs=[spec_f, spec_b],
        out_shape=[out, out],
        scratch_shapes=[pltpu.VMEM((LANES, LANES), F32), pltpu.VMEM((LANES, LANES), F32),
                        pltpu.VMEM((2 * nch, LANES, LANES), BF16),
                        pltpu.VMEM((2 * nch, LANES, LANES), F32),
                        pltpu.VMEM((2 * nch, LANES, LANES), BF16),
                        pltpu.VMEM((2 * nch, RW_CHUNK, LANES), F32),
                        pltpu.VMEM((2 * nch, 1, LANES), F32)],
        compiler_params=pltpu.CompilerParams(
            dimension_semantics=("parallel", "parallel", "arbitrary"),
            vmem_limit_bytes=VMEM_LIMIT),
        name="rwkv7_scan",
    )(k_a.reshape(1, w), msk, cum, r, k, v, kk, lw[0], a[0], r, k, v, kk, lw[1], a[1])


def _ssd_masks():
    i = np.arange(SSD_CHUNK)
    fwd = i[None, :] <= i[:, None]
    bwd = i[None, :] >= i[:, None]
    return np.stack([fwd, fwd.T, bwd, bwd.T]).astype(np.float32)


def _ssd_local(ci, refs, row0, reverse, msk_ref, scr):
    new_s, ecol_s, etot_s, cc_s, yd_s = scr
    dtm_ref, adtm_ref, adth_ref, x_ref, b_ref, c_ref = refs
    c = SSD_CHUNK
    d = 2 if reverse else 0
    rows = pl.ds(row0, c)
    dt_tm = dtm_ref[0, 0, 0, rows, :]
    adt_tm = adtm_ref[0, 0, 0, rows, :]
    adt_hm = adth_ref[0, 0, 0, :, rows]
    bc = b_ref[0, rows, :].astype(BF16)
    cc = c_ref[0, rows, :].astype(BF16)
    cc_s[ci] = cc
    mask = msk_ref[d]

    acum_tm = _dotf(mask, adt_tm, exact=True)
    acum_hm = _dotf(adt_hm, msk_ref[d + 1], exact=True)
    cb = _dot_nt(cc, bc)
    yield
    tot = acum_hm[:, 0:1] if reverse else acum_hm[:, c - 1:c]
    bt = bc.T
    lane = lax.broadcasted_iota(jnp.int32, (c, LANES), 1)
    first = lane < MB_HEAD
    lane1 = lax.broadcasted_iota(jnp.int32, (1, LANES), 1) < MB_HEAD

    for q in range(x_ref.shape[2] // LANES):
        e1, e2 = 2 * q, 2 * q + 1
        cols = slice(q * LANES, (q + 1) * LANES)
        col1 = jnp.broadcast_to(acum_tm[:, e1:e1 + 1], (c, LANES))
        col2 = jnp.broadcast_to(acum_tm[:, e2:e2 + 1], (c, LANES))
        colp = jnp.where(first, col1, col2)
        dtp = jnp.where(first, jnp.broadcast_to(dt_tm[:, e1:e1 + 1], (c, LANES)),
                        jnp.broadcast_to(dt_tm[:, e2:e2 + 1], (c, LANES)))
        totp = jnp.where(lane1, jnp.broadcast_to(tot[e1:e1 + 1, :], (1, LANES)),
                         jnp.broadcast_to(tot[e2:e2 + 1, :], (1, LANES)))
        xp = x_ref[0, rows, cols] * dtp
        xpb = xp.astype(BF16)
        l1 = jnp.exp(jnp.minimum(col1 - acum_hm[e1:e1 + 1, :], 0.0)) * mask
        l2 = jnp.exp(jnp.minimum(col2 - acum_hm[e2:e2 + 1, :], 0.0)) * mask
        yd1 = _dotf(l1 * cb, xpb)
        yd2 = _dotf(l2 * cb, xpb)
        ns = _dotf(bt, jnp.exp(totp - colp) * xp)
        ecol_s[ci, q] = jnp.exp(colp)
        etot_s[ci, q] = jnp.exp(totp)
        yield
        yd_s[ci, q] = jnp.where(first, yd1, yd2)
        new_s[ci, q] = ns


def _ssd_state_step(ci, y_ref, row0, s_ref, scr):
    new_s, ecol_s, etot_s, cc_s, yd_s = scr
    rows = pl.ds(row0, SSD_CHUNK)
    for q in range(s_ref.shape[0]):
        cols = slice(q * LANES, (q + 1) * LANES)
        st = s_ref[q]
        y = yd_s[ci, q] + ecol_s[ci, q] * _dotf(cc_s[ci], st)
        y_ref[0, rows, cols] = y.astype(y_ref.dtype)
        s_ref[q] = etot_s[ci, q] * st + new_s[ci, q]


def _ssd_kernel(msk_ref, *refs, n_chunks):
    fwd, bwd = refs[0:6], refs[6:12]
    yf_ref, yb_ref, sf_ref, sb_ref = refs[12:16]
    scr = refs[16:21]

    @pl.when(pl.program_id(2) == 0)
    def _():
        sf_ref[...] = jnp.zeros_like(sf_ref)
        sb_ref[...] = jnp.zeros_like(sb_ref)

    stages = []
    for j in range(n_chunks):
        stages.append(_ssd_local(j, fwd, j * SSD_CHUNK, False, msk_ref, scr))
        stages.append(_ssd_local(n_chunks + j, bwd, j * SSD_CHUNK, True, msk_ref, scr))
    _lockstep(stages)
    for j in range(n_chunks):
        _ssd_state_step(j, yf_ref, j * SSD_CHUNK, sf_ref, scr)
        jb = n_chunks - 1 - j
        _ssd_state_step(n_chunks + jb, yb_ref, jb * SSD_CHUNK, sb_ref, scr)


def _ssd_scan(xbc, dt, a_neg, *, tb=1024):
    bsz, t, _ = xbc.shape
    g = MB_GROUPS
    nh = dt.shape[-1]
    e = nh // g
    w = nh * MB_HEAD
    gw = e * MB_HEAD
    tb = min(tb, t)
    nb = t // tb
    assert t % tb == 0 and tb % SSD_CHUNK == 0 and MB_STATE == LANES
    dt_g = dt.reshape(bsz, t, 2, g, e)
    adt_g = dt_g * a_neg.reshape(2, g, e)
    dt_tm = jnp.transpose(dt_g, (2, 0, 3, 1, 4))
    adt_tm = jnp.transpose(adt_g, (2, 0, 3, 1, 4))
    adt_hm = jnp.transpose(adt_g, (2, 0, 3, 4, 1))

    def specs(d, tmap):
        return [
            pl.BlockSpec((1, 1, 1, tb, e), lambda b, gi, i: (d, b, gi, tmap(i), 0)),
            pl.BlockSpec((1, 1, 1, tb, e), lambda b, gi, i: (d, b, gi, tmap(i), 0)),
            pl.BlockSpec((1, 1, 1, e, tb), lambda b, gi, i: (d, b, gi, 0, tmap(i))),
            pl.BlockSpec((1, tb, gw), lambda b, gi, i: (b, tmap(i), gi)),
            pl.BlockSpec((1, tb, MB_STATE), lambda b, gi, i: (b, tmap(i), w // MB_STATE + gi)),
            pl.BlockSpec((1, tb, MB_STATE), lambda b, gi, i: (b, tmap(i), w // MB_STATE + g + gi)),
        ]

    fmap = lambda i: i
    bmap = lambda i: nb - 1 - i
    out = jax.ShapeDtypeStruct((bsz, t, w), BF16)
    n_pairs = gw // LANES
    nch = tb // SSD_CHUNK
    msk = jnp.asarray(_ssd_masks())
    return pl.pallas_call(
        functools.partial(_ssd_kernel, n_chunks=nch),
        grid=(bsz, g, nb),
        in_specs=[pl.BlockSpec(msk.shape, lambda b, gi, i: (0, 0, 0))] + specs(0, fmap) + specs(1, bmap),
        out_specs=[pl.BlockSpec((1, tb, gw), lambda b, gi, i: (b, i, gi)),
                   pl.BlockSpec((1, tb, gw), lambda b, gi, i: (b, nb - 1 - i, gi))],
        out_shape=[out, out],
        scratch_shapes=[pltpu.VMEM((n_pairs, MB_STATE, LANES), F32),
                        pltpu.VMEM((n_pairs, MB_STATE, LANES), F32),
                        pltpu.VMEM((2 * nch, n_pairs, MB_STATE, LANES), F32),
                        pltpu.VMEM((2 * nch, n_pairs, SSD_CHUNK, LANES), F32),
                        pltpu.VMEM((2 * nch, n_pairs, 1, LANES), F32),
                        pltpu.VMEM((2 * nch, SSD_CHUNK, MB_STATE), BF16),
                        pltpu.VMEM((2 * nch, n_pairs, SSD_CHUNK, LANES), F32)],
        compiler_params=pltpu.CompilerParams(
            dimension_semantics=("parallel", "parallel", "arbitrary"),
            vmem_limit_bytes=VMEM_LIMIT),
        name="ssd_scan",
    )(msk, dt_tm, adt_tm, adt_hm, xbc, xbc, xbc, dt_tm, adt_tm, adt_hm, xbc, xbc, xbc)


def _layer_norm(x, g, b):
    mu = jnp.mean(x, -1, keepdims=True)
    xc = x - mu
    var = jnp.mean(xc * xc, -1, keepdims=True)
    return xc * lax.rsqrt(var + LN_EPS) * g + b


def _pad_cols(wm, n):
    return jnp.pad(wm, ((0, 0), (0, n - wm.shape[1])))


def _pad_rows(wm, n):
    return jnp.pad(wm, ((0, n - wm.shape[0]), (0, 0)))


def _stack_rows(vecs, rows=8):
    a = jnp.stack([v.reshape(-1).astype(F32) for v in vecs])
    return _pad_rows(a, -(-a.shape[0] // rows) * rows)


HALO = 16


def _halo_specs(tm, tn, n_rows, col_off=0):
    last = n_rows // HALO - 1
    rt = tm // HALO
    return [pl.BlockSpec((tm, tn), lambda i, j: (i, col_off + j)),
            pl.BlockSpec((HALO, tn), lambda i, j: (jnp.maximum(i * rt - 1, 0), col_off + j)),
            pl.BlockSpec((HALO, tn), lambda i, j: (jnp.minimum((i + 1) * rt, last), col_off + j))]


def _stage_rows(ext_ref, x_ref, prev_ref, next_ref, tiles_per_seq):
    tm = x_ref.shape[0]
    i = pl.program_id(0)
    first = (i % tiles_per_seq) == 0
    last = (i % tiles_per_seq) == tiles_per_seq - 1
    ext_ref[0:HALO, :] = jnp.where(first, 0.0, prev_ref[...].astype(F32))
    ext_ref[HALO:HALO + tm, :] = x_ref[...].astype(F32)
    ext_ref[HALO + tm:2 * HALO + tm, :] = jnp.where(last, 0.0, next_ref[...].astype(F32))


def _shifted(ext_ref, s, tm):
    return ext_ref[pl.ds(HALO + s, tm), :]


def _delta(ext_ref, tm):
    return 0.5 * (_shifted(ext_ref, -1, tm) + _shifted(ext_ref, 1, tm)) - _shifted(ext_ref, 0, tm)


def _softplus(x):
    return jnp.maximum(x, 0.0) + jnp.log(1.0 + jnp.exp(-jnp.abs(x)))


def _group_sum(x, e_ref):
    e = e_ref[...]
    outs = []
    for s in range(x.shape[1] // LANES):
        xs = x[:, s * LANES:(s + 1) * LANES]
        hi = xs.astype(BF16)
        lo = (xs - hi.astype(F32)).astype(BF16)
        outs.append(jnp.dot(hi, e, preferred_element_type=F32)
                    + jnp.dot(lo, e, preferred_element_type=F32))
    return outs[0] if len(outs) == 1 else jnp.concatenate(outs, axis=1)


def _head_indicator():
    i = np.arange(LANES) // RW_HEAD
    return jnp.asarray((i[:, None] == i[None, :]).astype(np.float32), dtype=BF16)


def _row_params(tm, seq_len):
    tm = min(tm, seq_len)
    assert seq_len % tm == 0 and tm % HALO == 0
    return tm, seq_len // tm


def _lora_in_kernel(h_ref, hp_ref, hn_ref, mu_ref, w_ref, o_ref, ext_ref, *, tiles_per_seq, n_lora):
    tm = h_ref.shape[0]
    _stage_rows(ext_ref, h_ref, hp_ref, hn_ref, tiles_per_seq)
    h = h_ref[...]
    hd = _delta(ext_ref, tm)
    src = (0, 0, 1, 1, 2)
    for i in range(n_lora):
        x = (h + hd * mu_ref[src[i]:src[i] + 1, :]).astype(BF16)
        mid = jnp.dot(x, w_ref[:, i * LANES:(i + 1) * LANES], preferred_element_type=F32)
        if i < 2:
            mid = jnp.tanh(mid)
        o_ref[:, i * LANES:(i + 1) * LANES] = mid.astype(o_ref.dtype)


def _lora_in(h, mus, w_first, seq_len, *, tm=512):
    m, d = h.shape
    tm, tps = _row_params(tm, seq_len)
    n_lora = len(w_first)
    w = jnp.concatenate([_pad_cols(x, LANES) for x in w_first], axis=1).astype(BF16)
    mu = _stack_rows(mus)
    return pl.pallas_call(
        functools.partial(_lora_in_kernel, tiles_per_seq=tps, n_lora=n_lora),
        grid=(m // tm, 1),
        in_specs=_halo_specs(tm, d, m) + [pl.BlockSpec(mu.shape, lambda i, j: (0, 0)),
                                          pl.BlockSpec(w.shape, lambda i, j: (0, 0))],
        out_specs=pl.BlockSpec((tm, n_lora * LANES), lambda i, j: (i, 0)),
        out_shape=jax.ShapeDtypeStruct((m, n_lora * LANES), BF16),
        scratch_shapes=[pltpu.VMEM((tm + 2 * HALO, d), F32)],
        compiler_params=pltpu.CompilerParams(
            dimension_semantics=("parallel", "arbitrary"), vmem_limit_bytes=VMEM_LIMIT),
        name="lora_in",
    )(h, h, h, mu, w)


_P_MU_R, _P_MU_K, _P_MU_V, _P_W0, _P_A0, _P_KK, _P_KA, _P_RK, _P_V0 = 0, 1, 2, 3, 5, 7, 8, 9, 10


def _rwkv_prep_kernel(*refs, tiles_per_seq, has_vres):
    (pr, prp, prn, pk, pkp, pkn, pv, pvp, pvn, mid_ref, w2_ref, par_ref, e_ref) = refs[:13]
    pos = 13
    vf_ref = None
    if has_vres:
        vf_ref = refs[pos]
        pos += 1
    r_o, k_o, v_o, kk_o, lw0_o, lw1_o, a0_o, a1_o, bonus_o = refs[pos:pos + 9]
    er, ek, ev = refs[pos + 9:pos + 12]
    tm = pr.shape[0]
    par = lambda i: par_ref[i:i + 1, :]

    outs = []
    for ext, x, xp, xn, mu in ((er, pr, prp, prn, _P_MU_R), (ek, pk, pkp, pkn, _P_MU_K),
                               (ev, pv, pvp, pvn, _P_MU_V)):
        _stage_rows(ext, x, xp, xn, tiles_per_seq)
        outs.append(x[...] + par(mu) * _delta(ext, tm))
    r, k, v = outs

    def lora(i):
        return jnp.dot(mid_ref[:, i * LANES:(i + 1) * LANES], w2_ref[i],
                       preferred_element_type=F32)

    kdir = 0.0
    for d, (lw_o, a_o) in enumerate(((lw0_o, a0_o), (lw1_o, a1_o))):
        w_log = -_softplus(-(par(_P_W0 + d) + lora(d))) - 0.5
        lw_o[...] = -jnp.exp(w_log)
        a = _sigmoid(par(_P_A0 + d) + lora(2 + d))
        a_o[...] = a.astype(a_o.dtype)
        kdir = kdir + (1.0 + (a - 1.0) * par(_P_KA))
    if has_vres:
        vg = _sigmoid(par(_P_V0) + lora(4))
        v = v + (vf_ref[...].astype(F32) - v) * vg
    kk = k * par(_P_KK)
    nrm = jnp.maximum(jnp.sqrt(_group_sum(kk * kk, e_ref)), 1e-12)
    r_o[...] = r.astype(r_o.dtype)
    k_o[...] = k.astype(k_o.dtype)
    v_o[...] = v.astype(v_o.dtype)
    kk_o[...] = (kk / nrm).astype(kk_o.dtype)
    bonus_o[...] = (_group_sum(r * k * kdir * par(_P_RK), e_ref) * v).astype(bonus_o.dtype)


def _rwkv_prep(p_rkv, mid, w_second, params, v_first, seq_len, *, tm=512, tn=256):
    m, w3 = p_rkv.shape
    w = w3 // 3
    tm, tps = _row_params(tm, seq_len)
    tn = min(tn, w)
    nj = w // tn
    assert w % tn == 0 and tn % LANES == 0
    has_vres = v_first is not None
    n_lora = len(w_second)
    w2 = jnp.stack([_pad_rows(x, LANES) for x in w_second]).astype(BF16)
    par = _stack_rows(params)
    tile = pl.BlockSpec((tm, tn), lambda i, j: (i, j))
    in_specs = (_halo_specs(tm, tn, m, 0) + _halo_specs(tm, tn, m, nj) + _halo_specs(tm, tn, m, 2 * nj)
                + [pl.BlockSpec((tm, mid.shape[1]), lambda i, j: (i, 0)),
                   pl.BlockSpec((n_lora, LANES, tn), lambda i, j: (0, 0, j)),
                   pl.BlockSpec((par.shape[0], tn), lambda i, j: (0, j)),
                   pl.BlockSpec((LANES, LANES), lambda i, j: (0, 0))])
    args = [p_rkv] * 9 + [mid, w2, par, _head_indicator()]
    if has_vres:
        in_specs.append(tile)
        args.append(v_first)
    half = jax.ShapeDtypeStruct((m, w), BF16)
    full = jax.ShapeDtypeStruct((m, w), F32)
    return pl.pallas_call(
        functools.partial(_rwkv_prep_kernel, tiles_per_seq=tps, has_vres=has_vres),
        grid=(m // tm, nj),
        in_specs=in_specs,
        out_specs=[tile] * 9,
        out_shape=[half] * 4 + [full] * 2 + [half] * 3,
        scratch_shapes=[pltpu.VMEM((tm + 2 * HALO, tn), F32)] * 3,
        compiler_params=pltpu.CompilerParams(
            dimension_semantics=("parallel", "parallel"), vmem_limit_bytes=VMEM_LIMIT),
        name="rwkv_prep",
    )(*args)


def _branch_merge_kernel(yf_r, yb_r, bonus, g_r, par_r, e_ref, w_r,
                         yf_m, yb_m, xm, z, par_m, w_m, ga_ref, gb_ref,
                         o_ref, acc_r, acc_m, *, n_rw, gw):
    k = pl.program_id(1)

    @pl.when(k == 0)
    def _():
        acc_r[...] = jnp.zeros_like(acc_r)
        acc_m[...] = jnp.zeros_like(acc_m)

    @pl.when(k < n_rw)
    def _():
        inv_n = 1.0 / RW_HEAD
        y = yf_r[...].astype(F32) + yb_r[...].astype(F32)
        yc = y - _group_sum(y, e_ref) * inv_n
        var = _group_sum(yc * yc, e_ref) * inv_n
        gn = yc * lax.rsqrt(var + RW_GN_EPS) * par_r[0:1, :] + par_r[1:2, :]
        x = ((gn + bonus[...].astype(F32)) * g_r[...].astype(F32)).astype(BF16)
        acc_r[...] += jnp.dot(x, w_r[...], preferred_element_type=F32)

    @pl.when(k >= n_rw)
    def _():
        xs = []
        for s in range(yf_m.shape[1] // gw):
            c = slice(s * gw, (s + 1) * gw)
            y = ((yf_m[:, c].astype(F32) + yb_m[:, c].astype(F32) + par_m[0:1, c] * xm[:, c].astype(F32))
                 * z[:, c].astype(F32))
            ms = jnp.mean(y * y, axis=-1, keepdims=True)
            xs.append((y * lax.rsqrt(ms + RMS_EPS) * par_m[1:2, c]).astype(BF16))
        x = xs[0] if len(xs) == 1 else jnp.concatenate(xs, axis=1)
        acc_m[...] += jnp.dot(x, w_m[...], preferred_element_type=F32)

    @pl.when(k == pl.num_programs(1) - 1)
    def _():
        o_ref[...] = (ga_ref[...].astype(F32) * acc_r[...]
                      + gb_ref[...].astype(F32) * acc_m[...]).astype(o_ref.dtype)


def _branch_merge(y_f, y_b, bonus, g_rw, lnx_g, lnx_b, p_rw,
                  ys_f, ys_b, xbc_c, z, d_skip_rep, norm_g, p_mb, gates, *, tm=512, kw=512):
    m, rw_w = y_f.shape
    mb_w = z.shape[1]
    d = p_rw.shape[1]
    tm = min(tm, m)
    kr = min(kw, rw_w)
    gw = mb_w // MB_GROUPS
    km = 2 * gw
    n_rw, n_mb = rw_w // kr, mb_w // km
    assert rw_w % kr == 0 and kr % LANES == 0 and gates.shape[1] == 2 * d and MB_GROUPS % 2 == 0
    par_r = _stack_rows([lnx_g, lnx_b])
    par_m = _stack_rows([d_skip_rep, norm_g])
    rcol = lambda i, k: (i, jnp.minimum(k, n_rw - 1))
    mcol = lambda i, k: (i, jnp.maximum(k - n_rw, 0))
    rw_tile = pl.BlockSpec((tm, kr), rcol)
    mb_tile = pl.BlockSpec((tm, km), mcol)
    return pl.pallas_call(
        functools.partial(_branch_merge_kernel, n_rw=n_rw, gw=gw),
        grid=(m // tm, n_rw + n_mb),
        in_specs=[rw_tile] * 4
                 + [pl.BlockSpec((par_r.shape[0], kr), lambda i, k: (0, jnp.minimum(k, n_rw - 1))),
                    pl.BlockSpec((LANES, LANES), lambda i, k: (0, 0)),
                    pl.BlockSpec((kr, d), lambda i, k: (jnp.minimum(k, n_rw - 1), 0))]
                 + [mb_tile] * 4
                 + [pl.BlockSpec((par_m.shape[0], km), lambda i, k: (0, jnp.maximum(k - n_rw, 0))),
                    pl.BlockSpec((km, d), lambda i, k: (jnp.maximum(k - n_rw, 0), 0)),
                    pl.BlockSpec((tm, d), lambda i, k: (i, 0)),
                    pl.BlockSpec((tm, d), lambda i, k: (i, 1))],
        out_specs=pl.BlockSpec((tm, d), lambda i, k: (i, 0)),
        out_shape=jax.ShapeDtypeStruct((m, d), BF16),
        scratch_shapes=[pltpu.VMEM((tm, d), F32), pltpu.VMEM((tm, d), F32)],
        compiler_params=pltpu.CompilerParams(
            dimension_semantics=("parallel", "arbitrary"), vmem_limit_bytes=VMEM_LIMIT),
        name="branch_merge",
    )(y_f, y_b, bonus, g_rw, par_r, _head_indicator(), p_rw,
      ys_f, ys_b, xbc_c, z, par_m, p_mb, gates, gates)


CONV_BLK = 128
CONV_PAD = 64


def _conv_shift_matrix():
    taps = [j for j in range(CONV_W) if j != CONV_W // 2]
    s = np.zeros((len(taps) * CONV_BLK, 2 * CONV_BLK), np.float32)
    for i, j in enumerate(taps):
        t = np.arange(CONV_BLK)
        s[i * CONV_BLK + t, CONV_PAD + t + j - CONV_W // 2] = 1.0
    return taps, s


def _conv_silu_kernel(x_ref, xp_ref, xn_ref, w_ref, s_ref, o_ref, ext_ref, *, tiles_per_seq):
    tm = x_ref.shape[0]
    i = pl.program_id(0)
    first = (i % tiles_per_seq) == 0
    last = (i % tiles_per_seq) == tiles_per_seq - 1
    zero = jnp.zeros((CONV_PAD - HALO, ext_ref.shape[1]), ext_ref.dtype)
    ext_ref[0:CONV_PAD - HALO, :] = zero
    halo0 = jnp.zeros(xp_ref.shape, xp_ref.dtype)
    ext_ref[CONV_PAD - HALO:CONV_PAD, :] = jnp.where(first, halo0, xp_ref[...])
    ext_ref[CONV_PAD:CONV_PAD + tm, :] = x_ref[...]
    ext_ref[CONV_PAD + tm:CONV_PAD + tm + HALO, :] = jnp.where(last, halo0, xn_ref[...])
    ext_ref[CONV_PAD + tm + HALO:2 * CONV_PAD + tm, :] = zero
    taps, _ = _conv_shift_matrix()
    mid = CONV_W // 2
    for b in range(tm // CONV_BLK):
        rows = slice(b * CONV_BLK, (b + 1) * CONV_BLK)
        src = ext_ref[b * CONV_BLK:(b + 2) * CONV_BLK, :]
        sh = jnp.dot(s_ref[...], src, preferred_element_type=F32)
        acc = w_ref[CONV_W:CONV_W + 1, :] + x_ref[rows, :].astype(F32) * w_ref[mid:mid + 1, :]
        for n, j in enumerate(taps):
            acc = acc + sh[n * CONV_BLK:(n + 1) * CONV_BLK] * w_ref[j:j + 1, :]
        o_ref[rows, :] = (acc * _sigmoid(acc)).astype(o_ref.dtype)


def _conv_silu(x, conv_w, conv_b, seq_len, *, tm=512, tn=512):
    m, ch = x.shape
    tm, tps = _row_params(tm, seq_len)
    tn = min(tn, ch)
    assert ch % tn == 0 and CONV_W // 2 <= HALO <= CONV_PAD and tm % CONV_BLK == 0
    wb = jnp.concatenate([conv_w, conv_b[None, :]], axis=0)
    shift = jnp.asarray(_conv_shift_matrix()[1], dtype=x.dtype)
    return pl.pallas_call(
        functools.partial(_conv_silu_kernel, tiles_per_seq=tps),
        grid=(m // tm, ch // tn),
        in_specs=_halo_specs(tm, tn, m) + [pl.BlockSpec((CONV_W + 1, tn), lambda i, j: (0, j)),
                                           pl.BlockSpec(shift.shape, lambda i, j: (0, 0))],
        out_specs=pl.BlockSpec((tm, tn), lambda i, j: (i, j)),
        out_shape=jax.ShapeDtypeStruct((m, ch), BF16),
        scratch_shapes=[pltpu.VMEM((tm + 2 * CONV_PAD, tn), x.dtype)],
        compiler_params=pltpu.CompilerParams(
            dimension_semantics=("parallel", "parallel"), vmem_limit_bytes=VMEM_LIMIT),
        name="conv_silu",
    )(x, x, x, wb, shift)


def _out_ln_kernel(x_ref, w_ref, h_ref, par_ref, o_ref, ob_ref):
    out = jnp.dot(x_ref[...], w_ref[...], preferred_element_type=F32)
    res = ALPHA * h_ref[...] + out
    mu = jnp.mean(res, axis=-1, keepdims=True)
    xc = res - mu
    var = jnp.mean(xc * xc, axis=-1, keepdims=True)
    hn = xc * lax.rsqrt(var + LN_EPS) * par_ref[0:1, :] + par_ref[1:2, :]
    o_ref[...] = hn
    ob_ref[...] = hn.astype(ob_ref.dtype)


def _out_ln(mix, w_out, h, ln_g, ln_b, *, tm=256, rows=None):
    k = mix.shape[1]
    n = w_out.shape[1]
    r0, r1 = rows or (0, mix.shape[0])
    m = r1 - r0
    tm = min(tm, m)
    assert r0 % tm == 0 and m % tm == 0
    b0 = r0 // tm
    par = _stack_rows([ln_g, ln_b])
    return pl.pallas_call(
        _out_ln_kernel,
        grid=(m // tm,),
        in_specs=[pl.BlockSpec((tm, k), lambda i: (i + b0, 0)),
                  pl.BlockSpec((k, n), lambda i: (0, 0)),
                  pl.BlockSpec((tm, n), lambda i: (i + b0, 0)),
                  pl.BlockSpec(par.shape, lambda i: (0, 0))],
        out_specs=[pl.BlockSpec((tm, n), lambda i: (i, 0))] * 2,
        out_shape=[jax.ShapeDtypeStruct((m, n), F32), jax.ShapeDtypeStruct((m, n), BF16)],
        compiler_params=pltpu.CompilerParams(
            dimension_semantics=("parallel",), vmem_limit_bytes=VMEM_LIMIT),
        name="out_ln",
    )(mix, w_out, h, par)


def _trunk(x, ln_in_g, ln_in_b, w_in, mu_rkv, mu_wa, w0, w1, w2, a0, a1, a2, mu_v, v0, v1, v2,
           k_k, k_a, r_k, lnx_g, lnx_b, p_rw, conv_w, conv_b, dt_bias, a_log, d_skip, mb_norm_g,
           p_mb, w_out, ln_g, ln_b, out_batches=None):
    bsz, t, d = x.shape
    m = bsz * t
    out_batches = out_batches or (bsz,)
    rw_w = d
    mb_w = 2 * d
    rw_heads = rw_w // RW_HEAD
    mb_heads = mb_w // MB_HEAD
    conv_ch = mb_w + 2 * MB_GROUPS * MB_STATE
    splits = [3 * rw_w, 4 * rw_w, 4 * rw_w + mb_w, 4 * rw_w + mb_w + conv_ch,
              4 * rw_w + mb_w + conv_ch + 2 * mb_heads]
    seq = lambda a: a.reshape(bsz, t, a.shape[-1])

    h = _layer_norm(x, ln_in_g, ln_in_b).reshape(m, d)
    hb = h.astype(BF16)
    v_first = None
    for l in range(DEPTH):
        wl = w_in[l].astype(BF16)
        bounds = [0] + splits + [wl.shape[1]]
        outs = ((F32, None), (BF16, "silu"), (BF16, "silu"), (BF16, None), (F32, None), (BF16, "sigmoid"))
        p_rkv, g_rw, z, xbc, dt_raw, gates = [
            _matmul(hb, wl[:, bounds[i]:bounds[i + 1]], out_dtype=outs[i][0], act=outs[i][1])
            for i in range(6)]

        mus = [mu_wa[l, 0], mu_wa[l, 1]]
        w_first = [w1[l, 0], w1[l, 1], a1[l, 0], a1[l, 1]]
        w_second = [w2[l, 0], w2[l, 1], a2[l, 0], a2[l, 1]]
        params = [mu_rkv[l, 0], mu_rkv[l, 1], mu_rkv[l, 2], w0[l, 0], w0[l, 1], a0[l, 0], a0[l, 1],
                  k_k[l], k_a[l], r_k[l]]
        if l > 0:
            mus.append(mu_v[l - 1])
            w_first.append(v1[l - 1])
            w_second.append(v2[l - 1])
            params.append(v0[l - 1])
        mid = _lora_in(h, mus, w_first, t)
        r, k, v, kk, lw0, lw1, a_0, a_1, bonus = _rwkv_prep(p_rkv, mid, w_second, params, v_first, t)
        if l == 0:
            v_first = v
        y_f, y_b = _rwkv_scan(seq(r), seq(k), seq(v), seq(kk), (seq(lw0), seq(lw1)),
                              (seq(a_0), seq(a_1)), k_a[l])

        xbc_c = _conv_silu(xbc, conv_w[l], conv_b[l], t)
        dt = jax.nn.softplus(dt_raw.reshape(bsz, t, 2, mb_heads) + dt_bias[l])
        a_neg = -jnp.exp(a_log[l])
        ys_f, ys_b = _ssd_scan(seq(xbc_c), dt, a_neg)

        mix = _branch_merge(y_f.reshape(m, rw_w), y_b.reshape(m, rw_w), bonus, g_rw,
                            lnx_g[l], lnx_b[l], p_rw[l].astype(BF16),
                            ys_f.reshape(m, mb_w), ys_b.reshape(m, mb_w), xbc_c, z,
                            jnp.repeat(d_skip[l], MB_HEAD), mb_norm_g[l], p_mb[l].astype(BF16), gates)
        wo = w_out[l].astype(BF16)
        if l < DEPTH - 1:
            h, hb = _out_ln(mix, wo, h, ln_g[l], ln_b[l])
    bounds = [0]
    for nseq in out_batches:
        bounds.append(bounds[-1] + nseq * t)
    return [_out_ln(mix, wo, h, ln_g[-1], ln_b[-1], rows=(lo, hi))[0].reshape(-1, t, d)
            for lo, hi in zip(bounds[:-1], bounds[1:])]


def kernel(x_prompt, x_sample, ln_in_g, ln_in_b, w_in, mu_rkv, mu_wa, w0, w1, w2, a0, a1, a2, mu_v, v0, v1, v2, k_k, k_a, r_k, lnx_g, lnx_b, p_rw, conv_w, conv_b, dt_bias, a_log, d_skip, mb_norm_g, p_mb, w_out, ln_g, ln_b):
    assert x_prompt.shape[1:] == x_sample.shape[1:]
    nb = x_prompt.shape[0]
    x = jnp.concatenate([x_prompt, x_sample], axis=0)
    y_prompt, y_sample = _trunk(
        x, ln_in_g, ln_in_b, w_in, mu_rkv, mu_wa, w0, w1, w2, a0, a1, a2, mu_v, v0, v1, v2,
        k_k, k_a, r_k, lnx_g, lnx_b, p_rw, conv_w, conv_b, dt_bias, a_log, d_skip, mb_norm_g,
        p_mb, w_out, ln_g, ln_b, out_batches=(nb, x_sample.shape[0]))
    return (y_prompt.astype(x_prompt.dtype), y_sample.astype(x_sample.dtype))
```
